```python
import math
import jax, jax.numpy as jnp
from jax import lax
import numpy as np

D_MODEL = 1024
BATCH = 16
SEQ = 2048
DEPTH = 2

GRID_W = 64
CTX_LEN = 256
HEAD_DIM = 64
A_HEADS = 6
A_KV_HEADS = 2
A_GROUP = A_HEADS // A_KV_HEADS
WINDOW = 128
B_HEADS = 4
B_NOPE = 64
B_ROPE = 32
B_V = 64
B_Q_LORA = 256
B_KV_LORA = 128
C_HEADS = 6
C_KV_HEADS = 2
C_GROUP = C_HEADS // C_KV_HEADS
Q_BLOCK = 128
N_EXPERTS = 16
EXPERT_FF = 1024
CAPACITY_FACTOR = 2
N_BRANCHES = 3
ROPE_THETA = 10000.0
LN_EPS = 1e-5
RMS_EPS = 1e-6
ALPHA = (2 * DEPTH) ** 0.25
BETA = (8 * DEPTH) ** -0.25
A_SCALE = HEAD_DIM ** -0.5
B_SCALE = (B_NOPE + B_ROPE) ** -0.5
C_SCALE = HEAD_DIM ** -0.5
A_Q_W = A_HEADS * HEAD_DIM
A_KV_W = A_KV_HEADS * HEAD_DIM
C_Q_W = C_HEADS * HEAD_DIM
C_KV_W = C_KV_HEADS * HEAD_DIM
B_O_W = B_HEADS * B_V
IN_SIZES = (A_Q_W, A_KV_W, A_KV_W, B_Q_LORA, B_KV_LORA, B_ROPE, C_Q_W, C_KV_W, C_KV_W, N_BRANCHES * D_MODEL)
IN_W = sum(IN_SIZES)
BRANCH_W = A_Q_W + B_O_W + C_Q_W

kernel_name = "hybrid_diffusion_prefix_trunk"


def layer_norm(x, g, b):
    xf = x.astype(jnp.float32)
    mu = xf.mean(-1, keepdims=True)
    var = jnp.square(xf - mu).mean(-1, keepdims=True)
    return ((xf - mu) * lax.rsqrt(var + LN_EPS) * g + b).astype(x.dtype)


def rms_norm(x, g):
    xf = x.astype(jnp.float32)
    return (xf * lax.rsqrt(jnp.mean(xf * xf, -1, keepdims=True) + RMS_EPS) * g).astype(x.dtype)


def rope_1d(x, pos):
    half = x.shape[-1] // 2
    freqs = ROPE_THETA ** (-jnp.arange(half, dtype=jnp.float32) / half)
    ang = pos.astype(jnp.float32)[:, None] * freqs
    cos = jnp.cos(ang)[:, None, :]
    sin = jnp.sin(ang)[:, None, :]
    x1, x2 = x[..., :half], x[..., half:]
    return jnp.concatenate([x1 * cos - x2 * sin, x1 * sin + x2 * cos], -1).astype(x.dtype)


def axial_rope(x, row, col):
    a = x.shape[-1] // 2
    return jnp.concatenate([rope_1d(x[..., :a], row), rope_1d(x[..., a:], col)], -1)


def sink_softmax(scores, sink):
    m = jnp.maximum(scores.max(-1, keepdims=True), sink)
    e = jnp.exp(scores - m)
    return e / (e.sum(-1, keepdims=True) + jnp.exp(sink - m))


def windowed_sink_attention(q, k, v, ck, cv, sink):
    b, s, g, r, d = q.shape
    nblk = s // Q_BLOCK
    span = Q_BLOCK + 2 * WINDOW
    kp = jnp.pad(k, ((0, 0), (WINDOW, WINDOW), (0, 0), (0, 0)))
    vp = jnp.pad(v, ((0, 0), (WINDOW, WINDOW), (0, 0), (0, 0)))
    qb = jnp.moveaxis(q.reshape(b, nblk, Q_BLOCK, g, r, d), 1, 0)
    sink_b = sink.astype(jnp.float32).reshape(1, g, r, 1, 1)

    def block(args):
        i, qi = args
        start = i * Q_BLOCK
        ki = lax.dynamic_slice_in_dim(kp, start, span, axis=1)
        vi = lax.dynamic_slice_in_dim(vp, start, span, axis=1)
        qpos = start + jnp.arange(Q_BLOCK)
        kpos = start - WINDOW + jnp.arange(span)
        valid = (kpos[None, :] >= 0) & (kpos[None, :] < s) & (jnp.abs(qpos[:, None] - kpos[None, :]) <= WINDOW)
        s_loc = jnp.einsum('bqgrd,bkgd->bgrqk', qi, ki).astype(jnp.float32) * A_SCALE
        s_loc = jnp.where(valid, s_loc, -jnp.inf)
        s_ctx = jnp.einsum('bqgrd,bkgd->bgrqk', qi, ck).astype(jnp.float32) * A_SCALE
        p = sink_softmax(jnp.concatenate([s_loc, s_ctx], -1), sink_b).astype(v.dtype)
        return (jnp.einsum('bgrqk,bkgd->bqgrd', p[..., :span], vi)
                + jnp.einsum('bgrqk,bkgd->bqgrd', p[..., span:], cv))

    out = lax.map(block, (jnp.arange(nblk), qb))
    return jnp.moveaxis(out, 0, 1).reshape(b, s, g * r * d)


def sink_dense_attention(q, k, v, sink):
    b, n, g, r, d = q.shape
    sc = jnp.einsum('bqgrd,bkgd->bgrqk', q, k).astype(jnp.float32) * A_SCALE
    p = sink_softmax(sc, sink.astype(jnp.float32).reshape(1, g, r, 1, 1)).astype(v.dtype)
    return jnp.einsum('bgrqk,bkgd->bqgrd', p, v).reshape(b, n, g * r * d)


def dense_attention(q, k, v, scale):
    b, n, g, r, dk = q.shape
    nblk = n // Q_BLOCK
    qb = jnp.moveaxis(q.reshape(b, nblk, Q_BLOCK, g, r, dk), 1, 0)

    def block(qi):
        sc = jnp.einsum('bqgrd,bkgd->bgrqk', qi, k).astype(jnp.float32) * scale
        p = jax.nn.softmax(sc, axis=-1).astype(v.dtype)
        return jnp.einsum('bgrqk,bkgd->bqgrd', p, v)

    out = lax.map(block, qb)
    return jnp.moveaxis(out, 0, 1).reshape(b, n, g * r * v.shape[-1])


def project_heads(h, w_in, b_q_norm, b_kv_norm, b_w_uq, b_w_ukv, c_q_norm, c_k_norm, row, col):
    offs = [int(o) for o in np.cumsum(IN_SIZES)[:-1]]
    a_q, a_k, a_v, b_cq, b_ckv, b_kr, c_q, c_k, c_v, gate_logits = jnp.split(h @ w_in, offs, axis=-1)
    bsz, n, _ = h.shape
    heads = lambda t, nh: t.reshape(bsz, n, nh, HEAD_DIM)
    rot = (lambda t: axial_rope(t, row, col)) if row is not None else (lambda t: t)
    qa = rot(heads(a_q, A_HEADS)).reshape(bsz, n, A_KV_HEADS, A_GROUP, HEAD_DIM)
    ka = rot(heads(a_k, A_KV_HEADS))
    va = heads(a_v, A_KV_HEADS)
    qb = (rms_norm(b_cq, b_q_norm) @ b_w_uq).reshape(bsz, n, B_HEADS, B_NOPE + B_ROPE)
    qb = jnp.concatenate([qb[..., :B_NOPE], rot(qb[..., B_NOPE:])], -1)[:, :, :, None, :]
    kvb = (rms_norm(b_ckv, b_kv_norm) @ b_w_ukv).reshape(bsz, n, B_HEADS, B_NOPE + B_V)
    kr = jnp.broadcast_to(rot(b_kr[:, :, None, :]), (bsz, n, B_HEADS, B_ROPE))
    kb = jnp.concatenate([kvb[..., :B_NOPE], kr], -1)
    vb = kvb[..., B_NOPE:]
    qc = rot(rms_norm(heads(c_q, C_HEADS), c_q_norm)).reshape(bsz, n, C_KV_HEADS, C_GROUP, HEAD_DIM)
    kc = rot(rms_norm(heads(c_k, C_KV_HEADS), c_k_norm))
    vc = heads(c_v, C_KV_HEADS)
    return (qa, ka, va, qb, kb, vb, qc, kc, vc, gate_logits)


def merge_branches(o_a, o_b, o_c, gate_logits, w_branch, w_out):
    g_a, g_b, g_c = jnp.split(jax.nn.sigmoid(gate_logits), N_BRANCHES, axis=-1)
    p_a = o_a @ w_branch[:A_Q_W]
    p_b = o_b @ w_branch[A_Q_W:A_Q_W + B_O_W]
    p_c = o_c @ w_branch[A_Q_W + B_O_W:]
    return (g_a * p_a + g_b * p_b + g_c * p_c) @ w_out


def mix_latent(lat, cts, a_sink, w_branch, w_out):
    qa, ka, va, qb, kb, vb, qc, kc, vc, gl = lat
    _, cka, cva, _, ckb, cvb, _, ckc, cvc, _ = cts
    o_a = windowed_sink_attention(qa, ka, va, cka, cva, a_sink)
    o_b = dense_attention(qb, jnp.concatenate([ckb, kb], 1), jnp.concatenate([cvb, vb], 1), B_SCALE)
    o_c = dense_attention(qc, jnp.concatenate([ckc, kc], 1), jnp.concatenate([cvc, vc], 1), C_SCALE)
    return merge_branches(o_a, o_b, o_c, gl, w_branch, w_out)


def mix_context(cts, a_sink, w_branch, w_out):
    qa, ka, va, qb, kb, vb, qc, kc, vc, gl = cts
    o_a = sink_dense_attention(qa, ka, va, a_sink)
    o_b = dense_attention(qb, kb, vb, B_SCALE)
    o_c = dense_attention(qc, kc, vc, C_SCALE)
    return merge_branches(o_a, o_b, o_c, gl, w_branch, w_out)


def expert_choice_ffn(h, w_router, w_gate, w_up, w_down):
    b, n, d = h.shape
    cap = CAPACITY_FACTOR * n // N_EXPERTS
    aff = jax.nn.softmax((h @ w_router).astype(jnp.float32), axis=-1)
    top_aff, top_idx = lax.top_k(jnp.swapaxes(aff, 1, 2), cap)
    flat = top_idx.reshape(b, N_EXPERTS * cap)
    bidx = jnp.arange(b)[:, None]
    xs = h[bidx, flat].reshape(b, N_EXPERTS, cap, d)
    hid = jax.nn.silu(jnp.einsum('becd,edf->becf', xs, w_gate)) * jnp.einsum('becd,edf->becf', xs, w_up)
    ys = jnp.einsum('becf,efd->becd', hid, w_down) * top_aff[..., None].astype(h.dtype)
    return jnp.zeros_like(h).at[bidx, flat].add(ys.reshape(b, N_EXPERTS * cap, d))


def setup_inputs(seed: int = 0) -> dict:
    key = jax.random.key(seed)
    ks = jax.random.split(key, 24)
    nrm = lambda k, shape, scale: jax.random.normal(k, shape, jnp.float32) * scale
    gain = lambda k, shape: 1.0 + 0.02 * jax.random.normal(k, shape, jnp.float32)
    L = DEPTH
    return {
        "x": nrm(ks[0], (BATCH, SEQ, D_MODEL), 1.0),
        "c": nrm(ks[1], (BATCH, D_MODEL), 1.0),
        "ctx": nrm(ks[2], (BATCH, CTX_LEN, D_MODEL), 1.0),
        "c_ctx": nrm(ks[3], (D_MODEL,), 1.0),
        "w_ada": nrm(ks[4], (L, D_MODEL, 6 * D_MODEL), 0.5 * D_MODEL ** -0.5),
        "b_ada": nrm(ks[5], (L, 6 * D_MODEL), 0.01),
        "w_in": nrm(ks[6], (L, D_MODEL, IN_W), D_MODEL ** -0.5),
        "a_sink": nrm(ks[7], (L, A_HEADS), 0.5),
        "b_q_norm": gain(ks[8], (L, B_Q_LORA)),
        "b_kv_norm": gain(ks[9], (L, B_KV_LORA)),
        "b_w_uq": nrm(ks[10], (L, B_Q_LORA, B_HEADS * (B_NOPE + B_ROPE)), B_Q_LORA ** -0.5),
        "b_w_ukv": nrm(ks[11], (L, B_KV_LORA, B_HEADS * (B_NOPE + B_V)), B_KV_LORA ** -0.5),
        "c_q_norm": gain(ks[12], (L, HEAD_DIM)),
        "c_k_norm": gain(ks[13], (L, HEAD_DIM)),
        "w_branch": nrm(ks[14], (L, BRANCH_W, D_MODEL), A_Q_W ** -0.5),
        "w_out": nrm(ks[15], (L, D_MODEL, D_MODEL), BETA * D_MODEL ** -0.5),
        "ln1_g": gain(ks[16], (L, D_MODEL)),
        "ln1_b": nrm(ks[17], (L, D_MODEL), 0.01),
        "w_router": nrm(ks[18], (L, D_MODEL, N_EXPERTS), D_MODEL ** -0.5),
        "w_gate": nrm(ks[19], (L, N_EXPERTS, D_MODEL, EXPERT_FF), D_MODEL ** -0.5),
        "w_up": nrm(ks[20], (L, N_EXPERTS, D_MODEL, EXPERT_FF), D_MODEL ** -0.5),
        "w_down": nrm(ks[21], (L, N_EXPERTS, EXPERT_FF, D_MODEL), BETA * EXPERT_FF ** -0.5),
        "ln2_g": gain(ks[22], (L, D_MODEL)),
        "ln2_b": nrm(ks[23], (L, D_MODEL), 0.01),
    }


def reference(x, c, ctx, c_ctx, w_ada, b_ada, w_in, a_sink, b_q_norm, b_kv_norm, b_w_uq, b_w_ukv,
              c_q_norm, c_k_norm, w_branch, w_out, ln1_g, ln1_b, w_router, w_gate, w_up, w_down,
              ln2_g, ln2_b):
    n = x.shape[1]
    rows = n // GRID_W
    row = jnp.repeat(jnp.arange(rows), GRID_W)
    col = jnp.tile(jnp.arange(GRID_W), rows)
    cx = ctx
    for l in range(DEPTH):
        last = l == DEPTH - 1
        mod = (jax.nn.silu(c) @ w_ada[l] + b_ada[l])[:, None, :]
        mod_c = (jax.nn.silu(c_ctx) @ w_ada[l] + b_ada[l])[None, None, :]
        sh1, sc1, g1, sh2, sc2, g2 = jnp.split(mod, 6, axis=-1)
        sh1c, sc1c, g1c, sh2c, sc2c, g2c = jnp.split(mod_c, 6, axis=-1)
        h = x * (1.0 + sc1) + sh1
        hc = cx * (1.0 + sc1c) + sh1c
        lat = project_heads(h, w_in[l], b_q_norm[l], b_kv_norm[l], b_w_uq[l], b_w_ukv[l],
                            c_q_norm[l], c_k_norm[l], row, col)
        cts = project_heads(hc, w_in[l], b_q_norm[l], b_kv_norm[l], b_w_uq[l], b_w_ukv[l],
                            c_q_norm[l], c_k_norm[l], None, None)
        y = mix_latent(lat, cts, a_sink[l], w_branch[l], w_out[l])
        x = layer_norm(ALPHA * x + g1 * y, ln1_g[l], ln1_b[l])
        if not last:
            yc = mix_context(cts, a_sink[l], w_branch[l], w_out[l])
            cx = layer_norm(ALPHA * cx + g1c * yc, ln1_g[l], ln1_b[l])
        h = x * (1.0 + sc2) + sh2
        y = expert_choice_ffn(h, w_router[l], w_gate[l], w_up[l], w_down[l])
        x = layer_norm(ALPHA * x + g2 * y, ln2_g[l], ln2_b[l])
        if not last:
            hc = cx * (1.0 + sc2c) + sh2c
            yc = expert_choice_ffn(hc, w_router[l], w_gate[l], w_up[l], w_down[l])
            cx = layer_norm(ALPHA * cx + g2c * yc, ln2_g[l], ln2_b[l])
    return x
```

```python
import functools

import jax
import jax.numpy as jnp
import numpy as np
from jax import lax
from jax.experimental import pallas as pl
from jax.experimental.pallas import tpu as pltpu

F32 = jnp.float32
BF16 = jnp.bfloat16

GRID_W = 64
HEAD_DIM = 64
A_HEADS, A_KV_HEADS = 6, 2
WINDOW = 128
Q_BLOCK = 128
B_HEADS, B_NOPE, B_ROPE, B_V = 4, 64, 32, 64
B_Q_LORA, B_KV_LORA = 256, 128
C_HEADS, C_KV_HEADS = 6, 2
N_EXPERTS = 16
CAPACITY_FACTOR = 2
ROPE_THETA = 10000.0
LN_EPS = 1e-5
RMS_EPS = 1e-6
A_SCALE = HEAD_DIM ** -0.5
B_SCALE = (B_NOPE + B_ROPE) ** -0.5
C_SCALE = HEAD_DIM ** -0.5

LANES = 128
B_HEAD_PAD = 128
NEG_BIG = -1e30
VMEM_LIMIT = 56 * 1024 * 1024


def _params(*sem):
    return pltpu.CompilerParams(dimension_semantics=sem, vmem_limit_bytes=VMEM_LIMIT)


def _dot(a, b):
    return jnp.dot(a, b, preferred_element_type=F32)


def _dot_nt(a, b):
    return lax.dot_general(a, b, (((1,), (1,)), ((), ())), preferred_element_type=F32)


def _ada_body(c_ref, w_ref, b_ref, o_ref):
    c = c_ref[...]
    s = (c * (1.0 / (1.0 + jnp.exp(-c)))).astype(BF16)
    o_ref[0] = _dot(s, w_ref[0].astype(BF16)) + b_ref[0]


def _ada(cc, w_ada, b_ada):
    depth, d, six_d = w_ada.shape
    rows = cc.shape[0]
    tn = six_d // 4
    return pl.pallas_call(
        _ada_body,
        out_shape=jax.ShapeDtypeStruct((depth, rows, six_d), F32),
        grid=(depth, six_d // tn),
        in_specs=[
            pl.BlockSpec((rows, d), lambda l, j: (0, 0)),
            pl.BlockSpec((1, d, tn), lambda l, j: (l, 0, j)),
            pl.BlockSpec((1, 1, tn), lambda l, j: (l, 0, j)),
        ],
        out_specs=pl.BlockSpec((1, rows, tn), lambda l, j: (l, 0, j)),
        compiler_params=_params("arbitrary", "arbitrary"),
        name="ada",
    )(cc, w_ada, b_ada.reshape(depth, 1, six_d))


def _rope(x, cos, sin, d):
    lane = lax.broadcasted_iota(jnp.int32, (x.shape[0], LANES), 1)
    first = (lane // d) % 2 == 0
    outs = []
    for j in range(x.shape[1] // LANES):
        xs = x[:, j * LANES:(j + 1) * LANES]
        partner = jnp.where(first, pltpu.roll(xs, LANES - d, 1), pltpu.roll(xs, d, 1))
        outs.append(xs * cos + partner * sin)
    return outs[0] if len(outs) == 1 else jnp.concatenate(outs, axis=1)


def _rms(x, gain):
    return x * lax.rsqrt(jnp.mean(x * x, axis=-1, keepdims=True) + RMS_EPS) * gain


def _inproj_body(x_ref, sc_ref, sh_ref, w1_ref, w2_ref, w3_ref, w4_ref, w5_ref, wuq_ref, wuk_ref, bd_ref,
                 gq_ref, gkv_ref, gc_ref, cos_a_ref, sin_a_ref, cos_b_ref, sin_b_ref,
                 qa_ref, ka_ref, va_ref, qb_ref, kb_ref, vb_ref, qc_ref, kc_ref, vc_ref, gl_ref):
    h = (x_ref[0] * (1.0 + sc_ref[0]) + sh_ref[0]).astype(BF16)
    cos_a, sin_a = cos_a_ref[...], sin_a_ref[...]
    cos_b, sin_b = cos_b_ref[...], sin_b_ref[...]
    aq = A_HEADS * HEAD_DIM
    cq = C_HEADS * HEAD_DIM

    r = _rope(_dot(h, w1_ref[...]), cos_a, sin_a, HEAD_DIM // 4)
    qa_ref[0] = (r[:, :aq] * A_SCALE).astype(BF16)
    ka_ref[0] = r[:, aq:].astype(BF16)

    r = _dot(h, w2_ref[...])
    sq = r * r
    sq_hi = sq.astype(BF16)
    sq_lo = (sq - sq_hi.astype(F32)).astype(BF16)
    ms = _dot(sq_hi, bd_ref[...]) + _dot(sq_lo, bd_ref[...])
    r = _rope(r * lax.rsqrt(ms + RMS_EPS) * gc_ref[...], cos_a, sin_a, HEAD_DIM // 4)
    qc_ref[0] = (r[:, :cq] * C_SCALE).astype(BF16)
    kc_ref[0] = r[:, cq:].astype(BF16)

    r = _dot(h, w3_ref[...])
    va_ref[0] = r[:, :LANES].astype(BF16)
    vc_ref[0] = r[:, LANES:].astype(BF16)

    r = _dot(h, w4_ref[...])
    c_q = _rms(r[:, :B_Q_LORA], gq_ref[...]).astype(BF16)
    c_kv = _rms(r[:, B_Q_LORA:B_Q_LORA + B_KV_LORA], gkv_ref[...]).astype(BF16)
    k_rope = _rope(r[:, B_Q_LORA + B_KV_LORA:], cos_b, sin_b, B_ROPE // 4)
    q = _rope(_dot(c_q, wuq_ref[...]), cos_b, sin_b, B_ROPE // 4)
    qb_ref[0] = (q * B_SCALE).astype(BF16)
    kv = _dot(c_kv, wuk_ref[...])
    kw = B_HEADS * B_HEAD_PAD
    kb_ref[0] = (kv[:, :kw] + jnp.concatenate([k_rope] * B_HEADS, axis=1)).astype(BF16)
    vb_ref[0] = kv[:, kw:].astype(BF16)

    gl_ref[0] = _dot(h, w5_ref[...])


def _inproj(x, sc, sh, wts, tabs, tm):
    b, n, d = x.shape
    tm = min(tm, n)
    w1, w2, w3, w4, w5, wuq, wuk, bd, gq, gkv, gc = wts
    cos_a, sin_a, cos_b, sin_b = tabs
    full = lambda a: pl.BlockSpec(a.shape, lambda i, bb: (0,) * a.ndim)
    tok = lambda w: pl.BlockSpec((1, tm, w), lambda i, bb: (bb, i, 0))
    mod = pl.BlockSpec((1, 1, d), lambda i, bb: (bb, 0, 0))
    tab = pl.BlockSpec((tm, LANES), lambda i, bb: (i, 0))
    widths = (A_HEADS * HEAD_DIM, LANES, LANES, B_HEADS * B_HEAD_PAD, B_HEADS * B_HEAD_PAD, B_HEADS * B_V,
              C_HEADS * HEAD_DIM, LANES, LANES)
    out_shape = [jax.ShapeDtypeStruct((b, n, w), BF16) for w in widths]
    out_shape.append(jax.ShapeDtypeStruct((b, n, w5.shape[1]), F32))
    out_specs = [tok(w) for w in widths] + [tok(w5.shape[1])]
    return pl.pallas_call(
        _inproj_body,
        out_shape=out_shape,
        grid=(n // tm, b),
        in_specs=[tok(d), mod, mod] + [full(a) for a in wts] + [tab] * 4,
        out_specs=out_specs,
        compiler_params=_params("arbitrary", "arbitrary"),
        name="inproj",
    )(x, sc, sh, *wts, cos_a, sin_a, cos_b, sin_b)


def _softmax_pv(scores, values, sink):
    m = scores[0].max(axis=-1, keepdims=True)
    for s in scores[1:]:
        m = jnp.maximum(m, s.max(axis=-1, keepdims=True))
    if sink is not None:
        m = jnp.maximum(m, sink)
    den = jnp.exp(sink - m) if sink is not None else 0.0
    out = 0.0
    for s, v in zip(scores, values):
        e = jnp.exp(s - m)
        den = den + e.sum(axis=-1, keepdims=True)
        out = out + _dot(e.astype(BF16), v)
    return out / den


def _attn_body(*refs, n_seg, hq, hkv, dq, dv, use_sink):
    q_ref = refs[0]
    kv_refs = refs[1:1 + 2 * n_seg]
    sink_ref = refs[1 + 2 * n_seg] if use_sink else None
    o_ref = refs[-1]
    q = q_ref[0]
    rep = hq // hkv
    outs = []
    for g in range(hkv):
        ks = [kv_refs[2 * s][0][:, g * dq:(g + 1) * dq] for s in range(n_seg)]
        vs = [kv_refs[2 * s + 1][0][:, g * dv:(g + 1) * dv] for s in range(n_seg)]
        for r in range(rep):
            hd = g * rep + r
            qh = q[:, hd * dq:(hd + 1) * dq]
            scores = [_dot_nt(qh, k) for k in ks]
            outs.append(_softmax_pv(scores, vs, sink_ref[hd] if use_sink else None))
    o_ref[0] = jnp.concatenate(outs, axis=1).astype(BF16)


def _attn(q, segs, sink, *, hq, hkv, dq, dv, tq, name):
    b, n, _ = q.shape
    tq = min(tq, n)
    in_specs = [pl.BlockSpec((1, tq, hq * dq), lambda bb, i: (bb, i, 0))]
    args = [q]
    for k, v in segs:
        in_specs.append(pl.BlockSpec((1,) + k.shape[1:], lambda bb, i: (bb, 0, 0)))
        in_specs.append(pl.BlockSpec((1,) + v.shape[1:], lambda bb, i: (bb, 0, 0)))
        args += [k, v]
    if sink is not None:
        in_specs.append(pl.BlockSpec(memory_space=pltpu.SMEM))
        args.append(sink)
    body = functools.partial(_attn_body, n_seg=len(segs), hq=hq, hkv=hkv, dq=dq, dv=dv, use_sink=sink is not None)
    return pl.pallas_call(
        body,
        out_shape=jax.ShapeDtypeStruct((b, n, hq * dv), BF16),
        grid=(b, n // tq),
        in_specs=in_specs,
        out_specs=pl.BlockSpec((1, tq, hq * dv), lambda bb, i: (bb, i, 0)),
        compiler_params=_params("arbitrary", "arbitrary"),
        name=name,
    )(*args)


def _attn_a_body(q_ref, k_ref, v_ref, ck_ref, cv_ref, sink_ref, o_ref, *, seq):
    i = pl.program_id(1)
    start = pl.multiple_of(i * Q_BLOCK, Q_BLOCK)
    span = Q_BLOCK + 2 * WINDOW
    q = q_ref[0]
    k_win = k_ref[0, pl.ds(start, span), :]
    v_win = v_ref[0, pl.ds(start, span), :]
    ck, cv = ck_ref[0], cv_ref[0]
    qpos = start + lax.broadcasted_iota(jnp.int32, (Q_BLOCK, span), 0)
    kpos = start - WINDOW + lax.broadcasted_iota(jnp.int32, (Q_BLOCK, span), 1)
    valid = (kpos >= 0) & (kpos < seq) & (jnp.abs(qpos - kpos) <= WINDOW)
    rep = A_HEADS // A_KV_HEADS
    outs = []
    for g in range(A_KV_HEADS):
        sl = slice(g * HEAD_DIM, (g + 1) * HEAD_DIM)
        for r in range(rep):
            hd = g * rep + r
            qh = q[:, hd * HEAD_DIM:(hd + 1) * HEAD_DIM]
            s_loc = jnp.where(valid, _dot_nt(qh, k_win[:, sl]), NEG_BIG)
            s_ctx = _dot_nt(qh, ck[:, sl])
            outs.append(_softmax_pv([s_loc, s_ctx], [v_win[:, sl], cv[:, sl]], sink_ref[hd]))
    o_ref[0] = jnp.concatenate(outs, axis=1).astype(BF16)


def _attn_a(q, k, v, ck, cv, sink):
    b, n, w = q.shape
    kp = jnp.pad(k, ((0, 0), (WINDOW, WINDOW), (0, 0)))
    vp = jnp.pad(v, ((0, 0), (WINDOW, WINDOW), (0, 0)))
    whole = lambda a: pl.BlockSpec((1,) + a.shape[1:], lambda bb, i: (bb, 0, 0))
    return pl.pallas_call(
        functools.partial(_attn_a_body, seq=n),
        out_shape=jax.ShapeDtypeStruct((b, n, w), BF16),
        grid=(b, n // Q_BLOCK),
        in_specs=[pl.BlockSpec((1, Q_BLOCK, w), lambda bb, i: (bb, i, 0)), whole(kp), whole(vp), whole(ck), whole(cv),
                  pl.BlockSpec(memory_space=pltpu.SMEM)],
        out_specs=pl.BlockSpec((1, Q_BLOCK, w), lambda bb, i: (bb, i, 0)),
        compiler_params=_params("arbitrary", "arbitrary"),
        name="attn_a_window",
    )(q, kp, vp, ck, cv, sink)


def _sigmoid(z):
    return 1.0 / (1.0 + jnp.exp(-z))


def _layer_norm(z, g, b):
    mu = jnp.mean(z, axis=-1, keepdims=True)
    zc = z - mu
    var = jnp.mean(zc * zc, axis=-1, keepdims=True)
    return zc * lax.rsqrt(var + LN_EPS) * g + b


def _merge_body(oa_ref, ob_ref, oc_ref, gl_ref, x_ref, g1_ref, sc2_ref, sh2_ref, lng_ref, lnb_ref,
                wa_ref, wb_ref, wc_ref, wo_ref, wr_ref, x1_ref, h2_ref, aff_ref, *, alpha):
    d = x_ref.shape[-1]
    gl = gl_ref[0]
    merged = _sigmoid(gl[:, :d]) * _dot(oa_ref[0], wa_ref[...])
    merged = merged + _sigmoid(gl[:, d:2 * d]) * _dot(ob_ref[0], wb_ref[...])
    merged = merged + _sigmoid(gl[:, 2 * d:]) * _dot(oc_ref[0], wc_ref[...])
    y = _dot(merged.astype(BF16), wo_ref[...])
    x1 = _layer_norm(alpha * x_ref[0] + g1_ref[0] * y, lng_ref[...], lnb_ref[...])
    x1_ref[0] = x1
    h2 = (x1 * (1.0 + sc2_ref[0]) + sh2_ref[0]).astype(BF16)
    h2_ref[0] = h2
    logits = _dot(h2, wr_ref[...])
    lane = lax.broadcasted_iota(jnp.int32, logits.shape, 1)
    logits = jnp.where(lane < N_EXPERTS, logits, NEG_BIG)
    e = jnp.exp(logits - logits.max(axis=-1, keepdims=True))
    aff_ref[0] = e / e.sum(axis=-1, keepdims=True)


def _merge(oa, ob, oc, gl, x, g1, sc2, sh2, lng, lnb, wa, wb, wc, wo, wr, alpha, tm):
    b, n, d = x.shape
    tm = min(tm, n)
    tok = lambda w: pl.BlockSpec((1, tm, w), lambda bb, i: (bb, i, 0))
    mod = pl.BlockSpec((1, 1, d), lambda bb, i: (bb, 0, 0))
    full = lambda a: pl.BlockSpec(a.shape, lambda bb, i: (0,) * a.ndim)
    return pl.pallas_call(
        functools.partial(_merge_body, alpha=alpha),
        out_shape=[jax.ShapeDtypeStruct((b, n, d), F32), jax.ShapeDtypeStruct((b, n, d), BF16),
                   jax.ShapeDtypeStruct((b, n, LANES), F32)],
        grid=(b, n // tm),
        in_specs=[tok(oa.shape[2]), tok(ob.shape[2]), tok(oc.shape[2]), tok(gl.shape[2]), tok(d), mod, mod, mod,
                  full(lng), full(lnb), full(wa), full(wb), full(wc), full(wo), full(wr)],
        out_specs=[tok(d), tok(d), tok(LANES)],
        compiler_params=_params("arbitrary", "arbitrary"),
        name="merge",
    )(oa, ob, oc, gl, x, g1, sc2, sh2, lng, lnb, wa, wb, wc, wo, wr)


def _route_body(aff_ref, tri_ref, slot_ref, *, cap):
    bits = lax.bitcast_convert_type(aff_ref[...], jnp.int32)
    count = lambda m: jnp.sum(jnp.where(m, 1.0, 0.0), axis=1, keepdims=True)
    thr = jnp.zeros((bits.shape[0], 1), jnp.int32)
    for bit in range(30, -1, -1):
        cand = thr | (1 << bit)
        thr = jnp.where(count(bits >= cand) >= cap, cand, thr)
    above = bits > thr
    tie = bits == thr
    need = cap - count(above)
    tri = tri_ref[...]
    tie_rank = _dot(jnp.where(tie, 1.0, 0.0).astype(BF16), tri)
    sel = above | (tie & (tie_rank < need))
    pos = _dot(jnp.where(sel, 1.0, 0.0).astype(BF16), tri)
    slot_ref[...] = jnp.where(sel, pos.astype(jnp.int32), -1)


def _route(aff_t, cap):
    rows, n = aff_t.shape
    tri = jnp.triu(jnp.ones((n, n), BF16), k=1)
    return pl.pallas_call(
        functools.partial(_route_body, cap=cap),
        out_shape=jax.ShapeDtypeStruct((rows, n), jnp.int32),
        compiler_params=pltpu.CompilerParams(vmem_limit_bytes=VMEM_LIMIT),
        name="route",
    )(aff_t, tri)


def _experts_body(h_ref, slot_row_ref, aff_row_ref, slot_tok_ref, wg_ref, wu_ref, wd_ref, o_ref, *, cap):
    e = pl.program_id(1)
    n = h_ref.shape[1]
    slot_row = slot_row_ref[0, 0]
    pick = slot_row == lax.broadcasted_iota(jnp.int32, (cap, n), 0)
    xs = _dot(jnp.where(pick, 1.0, 0.0).astype(BF16), h_ref[0]).astype(BF16)
    hid = _dot(xs, wg_ref[0])
    hid = hid * _sigmoid(hid) * _dot(xs, wu_ref[0])
    ys = _dot(hid.astype(BF16), wd_ref[0])
    aff = jnp.sum(jnp.where(pick, aff_row_ref[0, 0], 0.0), axis=1, keepdims=True)
    ys = (ys * aff).astype(BF16)
    onehot = jnp.where(lax.broadcasted_iota(jnp.int32, (LANES, LANES), 0) == e, 1.0, 0.0).astype(BF16)
    slot_col = _dot(slot_tok_ref[0], onehot)
    parts = []
    for j in range(0, cap, LANES):
        w = min(LANES, cap - j)
        lane = (lax.broadcasted_iota(jnp.int32, (n, w), 1) + j).astype(F32)
        parts.append(jnp.where(slot_col[:, :w] == lane, 1.0, 0.0).astype(BF16))
    put = parts[0] if len(parts) == 1 else jnp.concatenate(parts, axis=1)
    upd = _dot(put, ys)

    @pl.when(e == 0)
    def _():
        o_ref[0] = upd

    @pl.when(e != 0)
    def _():
        o_ref[0] += upd


def _experts(h2, slot_t, aff_t, wg, wu, wd, cap):
    b, n, d = h2.shape
    ne, _, f = wg.shape
    slot_rows = slot_t.reshape(b, ne, 1, n)
    aff_rows = aff_t.reshape(b, ne, 1, n)
    slot_tok = jnp.transpose(slot_t.reshape(b, ne, n), (0, 2, 1)).astype(BF16)
    slot_tok = jnp.pad(slot_tok, ((0, 0), (0, 0), (0, LANES - ne)))
    row = pl.BlockSpec((1, 1, 1, n), lambda bb, e: (bb, e, 0, 0))
    return pl.pallas_call(
        functools.partial(_experts_body, cap=cap),
        out_shape=jax.ShapeDtypeStruct((b, n, d), F32),
        grid=(b, ne),
        in_specs=[pl.BlockSpec((1, n, d), lambda bb, e: (bb, 0, 0)), row, row,
                  pl.BlockSpec((1, n, LANES), lambda bb, e: (bb, 0, 0)),
                  pl.BlockSpec((1, d, f), lambda bb, e: (e, 0, 0)),
                  pl.BlockSpec((1, d, f), lambda bb, e: (e, 0, 0)),
                  pl.BlockSpec((1, f, d), lambda bb, e: (e, 0, 0))],
        out_specs=pl.BlockSpec((1, n, d), lambda bb, e: (bb, 0, 0)),
        compiler_params=_params("arbitrary", "arbitrary"),
        name="experts",
    )(h2, slot_rows, aff_rows, slot_tok, wg, wu, wd)


def _moe(h2, aff, wg, wu, wd):
    b, n, _ = h2.shape
    cap = CAPACITY_FACTOR * n // N_EXPERTS
    aff_t = jnp.transpose(aff[:, :, :N_EXPERTS], (0, 2, 1)).reshape(b * N_EXPERTS, n)
    slot_t = _route(aff_t, cap)
    return _experts(h2, slot_t, aff_t, wg, wu, wd, cap)


def _ln2_body(x_ref, y_ref, g2_ref, lng_ref, lnb_ref, o_ref, *, alpha):
    o_ref[0] = _layer_norm(alpha * x_ref[0] + g2_ref[0] * y_ref[0], lng_ref[...], lnb_ref[...])


def _ln2(x1, y, g2, lng, lnb, alpha, tm):
    b, n, d = x1.shape
    tm = min(tm, n)
    tok = pl.BlockSpec((1, tm, d), lambda bb, i: (bb, i, 0))
    vec = pl.BlockSpec((1, d), lambda bb, i: (0, 0))
    return pl.pallas_call(
        functools.partial(_ln2_body, alpha=alpha),
        out_shape=jax.ShapeDtypeStruct((b, n, d), F32),
        grid=(b, n // tm),
        in_specs=[tok, tok, pl.BlockSpec((1, 1, d), lambda bb, i: (bb, 0, 0)), vec, vec],
        out_specs=tok,
        compiler_params=_params("arbitrary", "arbitrary"),
        name="ln2",
    )(x1, y, g2, lng, lnb)


def _rope_tables(n, enabled):
    lane = np.arange(LANES)
    if not enabled:
        one, zero = jnp.ones((n, LANES), F32), jnp.zeros((n, LANES), F32)
        return one, zero, one, zero
    t = jnp.arange(n)
    pos = jnp.stack([t // GRID_W, t % GRID_W], axis=1).astype(F32)

    def build(jj, d, active):
        freqs = ROPE_THETA ** (-jnp.arange(d, dtype=F32) / d)
        ang = pos[:, (jj // (2 * d)) % 2] * freqs[jj % d][None, :]
        sign = np.where((jj // d) % 2 == 0, -1.0, 1.0).astype(np.float32)
        cos = jnp.where(active[None, :], jnp.cos(ang), 1.0)
        sin = jnp.where(active[None, :], jnp.sin(ang) * sign[None, :], 0.0)
        return cos, sin

    cos_a, sin_a = build(lane % HEAD_DIM, HEAD_DIM // 4, np.ones(LANES, bool))
    in_rope = (lane >= B_NOPE) & (lane < B_NOPE + B_ROPE)
    cos_b, sin_b = build(np.where(in_rope, lane - B_NOPE, 0), B_ROPE // 4, in_rope)
    return cos_a, sin_a, cos_b, sin_b


def _layer_weights(w_in, b_q_norm, b_kv_norm, b_w_uq, b_w_ukv, c_q_norm, c_k_norm):
    d = w_in.shape[0]
    sizes = (A_HEADS * HEAD_DIM, A_KV_HEADS * HEAD_DIM, A_KV_HEADS * HEAD_DIM, B_Q_LORA, B_KV_LORA, B_ROPE,
             C_HEADS * HEAD_DIM, C_KV_HEADS * HEAD_DIM, C_KV_HEADS * HEAD_DIM)
    offs = np.cumsum((0,) + sizes)
    a_q, a_k, a_v, b_cq, b_ckv, b_kr, c_q, c_k, c_v = [w_in[:, offs[i]:offs[i + 1]] for i in range(9)]
    gate = w_in[:, offs[9]:]
    zeros = lambda w: jnp.zeros((d, w), w_in.dtype)
    w1 = jnp.concatenate([a_q, a_k], axis=1)
    w2 = jnp.concatenate([c_q, c_k], axis=1)
    w3 = jnp.concatenate([a_v, c_v], axis=1)
    w4 = jnp.concatenate([b_cq, b_ckv, zeros(B_NOPE), b_kr, zeros(B_HEAD_PAD - B_NOPE - B_ROPE)], axis=1)
    uq = b_w_uq.reshape(B_Q_LORA, B_HEADS, B_NOPE + B_ROPE)
    uq = jnp.pad(uq, ((0, 0), (0, 0), (0, B_HEAD_PAD - B_NOPE - B_ROPE))).reshape(B_Q_LORA, B_HEADS * B_HEAD_PAD)
    ukv = b_w_ukv.reshape(B_KV_LORA, B_HEADS, B_NOPE + B_V)
    uk = jnp.pad(ukv[:, :, :B_NOPE], ((0, 0), (0, 0), (0, B_HEAD_PAD - B_NOPE))).reshape(B_KV_LORA, B_HEADS * B_HEAD_PAD)
    uv = ukv[:, :, B_NOPE:].reshape(B_KV_LORA, B_HEADS * B_V)
    wuk = jnp.concatenate([uk, uv], axis=1)
    width = w2.shape[1]
    blk = np.arange(width) // HEAD_DIM
    bd = jnp.asarray((blk[:, None] == blk[None, :]).astype(np.float32) / HEAD_DIM, BF16)
    gc = jnp.concatenate([jnp.tile(c_q_norm, C_HEADS), jnp.tile(c_k_norm, C_KV_HEADS)])[None, :]
    cast = lambda a: a.astype(BF16)
    return (cast(w1), cast(w2), cast(w3), cast(w4), cast(gate), cast(uq), cast(wuk), bd,
            b_q_norm[None, :], b_kv_norm[None, :], gc)


def kernel(x, c, ctx, c_ctx, w_ada, b_ada, w_in, a_sink, b_q_norm, b_kv_norm, b_w_uq, b_w_ukv, c_q_norm, c_k_norm,
           w_branch, w_out, ln1_g, ln1_b, w_router, w_gate, w_up, w_down, ln2_g, ln2_b):
    bsz, n, d = x.shape
    depth = w_ada.shape[0]
    alpha = (2 * depth) ** 0.25
    aq, bo = A_HEADS * HEAD_DIM, B_HEADS * B_V
    tm = 256

    rows = -(-(bsz + 1) // 8) * 8
    cc = jnp.concatenate([c, c_ctx[None, :], jnp.zeros((rows - bsz - 1, d), c.dtype)], axis=0)
    mod = _ada(cc, w_ada, b_ada)
    tabs_lat = _rope_tables(n, True)
    tabs_ctx = _rope_tables(ctx.shape[1], False)

    b_kw = dict(hq=B_HEADS, hkv=B_HEADS, dq=B_HEAD_PAD, dv=B_V, tq=256)
    c_kw = dict(hq=C_HEADS, hkv=C_KV_HEADS, dq=HEAD_DIM, dv=HEAD_DIM, tq=256)
    a_kw = dict(hq=A_HEADS, hkv=A_KV_HEADS, dq=HEAD_DIM, dv=HEAD_DIM, tq=256)

    cx = ctx
    for l in range(depth):
        last = l == depth - 1
        lat_mod = [m[:, None, :] for m in jnp.split(mod[l, :bsz], 6, axis=-1)]
        ctx_mod = [jnp.broadcast_to(m[None, None, :], (bsz, 1, d)) for m in jnp.split(mod[l, bsz], 6, axis=-1)]
        sh1, sc1, g1, sh2, sc2, g2 = lat_mod
        sh1c, sc1c, g1c, sh2c, sc2c, g2c = ctx_mod
        wts = _layer_weights(w_in[l], b_q_norm[l], b_kv_norm[l], b_w_uq[l], b_w_ukv[l], c_q_norm[l], c_k_norm[l])
        wbr = w_branch[l].astype(BF16)
        merge_w = (wbr[:aq], wbr[aq:aq + bo], wbr[aq + bo:], w_out[l].astype(BF16),
                   jnp.pad(w_router[l], ((0, 0), (0, LANES - N_EXPERTS))).astype(BF16))
        ffn_w = (w_gate[l].astype(BF16), w_up[l].astype(BF16), w_down[l].astype(BF16))
        ln1 = (ln1_g[l][None, :], ln1_b[l][None, :])
        ln2 = (ln2_g[l][None, :], ln2_b[l][None, :])

        qa, ka, va, qb, kb, vb, qc, kc, vc, gl = _inproj(x, sc1, sh1, wts, tabs_lat, tm)
        cqa, cka, cva, cqb, ckb, cvb, cqc, ckc, cvc, cgl = _inproj(cx, sc1c, sh1c, wts, tabs_ctx, tm)

        o_a = _attn_a(qa, ka, va, cka, cva, a_sink[l])
        o_b = _attn(qb, [(ckb, cvb), (kb, vb)], None, name="attn_b", **b_kw)
        o_c = _attn(qc, [(ckc, cvc), (kc, vc)], None, name="attn_c", **c_kw)
        x1, h2, aff = _merge(o_a, o_b, o_c, gl, x, g1, sc2, sh2, *ln1, *merge_w, alpha, tm)
        if not last:
            oc_a = _attn(cqa, [(cka, cva)], a_sink[l], name="attn_a_ctx", **a_kw)
            oc_b = _attn(cqb, [(ckb, cvb)], None, name="attn_b_ctx", **b_kw)
            oc_c = _attn(cqc, [(ckc, cvc)], None, name="attn_c_ctx", **c_kw)
            cx1, hc2, affc = _merge(oc_a, oc_b, oc_c, cgl, cx, g1c, sc2c, sh2c, *ln1, *merge_w, alpha, tm)

        x = _ln2(x1, _moe(h2, aff, *ffn_w), g2, *ln2, alpha, tm)
        if not last:
            cx = _ln2(cx1, _moe(hc2, affc, *ffn_w), g2c, *ln2, alpha, tm)
    return x
```

```python
import functools

import jax
import jax.numpy as jnp
import numpy as np
from jax import lax
from jax.experimental import pallas as pl
from jax.experimental.pallas import tpu as pltpu

F32 = jnp.float32
BF16 = jnp.bfloat16

GRID_W = 64
HEAD_DIM = 64
A_HEADS, A_KV_HEADS = 6, 2
WINDOW = 128
Q_BLOCK = 128
B_HEADS, B_NOPE, B_ROPE, B_V = 4, 64, 32, 64
B_Q_LORA, B_KV_LORA = 256, 128
C_HEADS, C_KV_HEADS = 6, 2
N_EXPERTS = 16
CAPACITY_FACTOR = 2
ROPE_THETA = 10000.0
LN_EPS = 1e-5
RMS_EPS = 1e-6
A_SCALE = HEAD_DIM ** -0.5
B_SCALE = (B_NOPE + B_ROPE) ** -0.5
C_SCALE = HEAD_DIM ** -0.5

LANES = 128
MXU_ROWS = 256
B_HEAD_PAD = 128
NEG_BIG = -1e30
VMEM_LIMIT = 56 * 1024 * 1024


def _params(*sem):
    return pltpu.CompilerParams(dimension_semantics=sem, vmem_limit_bytes=VMEM_LIMIT)


def _dot(a, b):
    return jnp.dot(a, b, preferred_element_type=F32)


def _dot_nt(a, b):
    return lax.dot_general(a, b, (((1,), (1,)), ((), ())), preferred_element_type=F32)


def _ada_body(c_ref, w_ref, b_ref, o_ref):
    c = c_ref[...]
    s = (c * (1.0 / (1.0 + jnp.exp(-c)))).astype(BF16)
    o_ref[0] = _dot(s, w_ref[0].astype(BF16)) + b_ref[0]


def _ada(cc, w_ada, b_ada):
    depth, d, six_d = w_ada.shape
    rows = cc.shape[0]
    tn = six_d // 4
    return pl.pallas_call(
        _ada_body,
        out_shape=jax.ShapeDtypeStruct((depth, rows, six_d), F32),
        grid=(depth, six_d // tn),
        in_specs=[
            pl.BlockSpec((rows, d), lambda l, j: (0, 0)),
            pl.BlockSpec((1, d, tn), lambda l, j: (l, 0, j)),
            pl.BlockSpec((1, 1, tn), lambda l, j: (l, 0, j)),
        ],
        out_specs=pl.BlockSpec((1, rows, tn), lambda l, j: (l, 0, j)),
        compiler_params=_params("arbitrary", "arbitrary"),
        name="ada",
    )(cc, w_ada, b_ada.reshape(depth, 1, six_d))


def _rope(x, cos, sin, d):
    lane = lax.broadcasted_iota(jnp.int32, (x.shape[0], LANES), 1)
    first = (lane // d) % 2 == 0
    outs = []
    for j in range(x.shape[1] // LANES):
        xs = x[:, j * LANES:(j + 1) * LANES]
        partner = jnp.where(first, pltpu.roll(xs, LANES - d, 1), pltpu.roll(xs, d, 1))
        outs.append(xs * cos + partner * sin)
    return outs[0] if len(outs) == 1 else jnp.concatenate(outs, axis=1)


def _rms(x, gain):
    return x * lax.rsqrt(jnp.mean(x * x, axis=-1, keepdims=True) + RMS_EPS) * gain


def _inproj_body(x_ref, sc_ref, sh_ref, w1_ref, w2_ref, w3_ref, w4_ref, wuq_ref, wuk_ref, bd_ref,
                 gq_ref, gkv_ref, gc_ref, cos_a_ref, sin_a_ref, cos_b_ref, sin_b_ref,
                 qa_ref, ka_ref, va_ref, qb_ref, kb_ref, vb_ref, qc_ref, kc_ref, vc_ref):
    h = (x_ref[0] * (1.0 + sc_ref[0]) + sh_ref[0]).astype(BF16)
    cos_a, sin_a = cos_a_ref[...], sin_a_ref[...]
    cos_b, sin_b = cos_b_ref[...], sin_b_ref[...]
    aq = A_HEADS * HEAD_DIM
    cq = C_HEADS * HEAD_DIM

    r = _rope(_dot(h, w1_ref[...]), cos_a, sin_a, HEAD_DIM // 4)
    qa_ref[0] = (r[:, :aq] * A_SCALE).astype(BF16)
    ka_ref[0] = r[:, aq:].astype(BF16)

    r = _dot(h, w2_ref[...])
    sq = r * r
    sq_hi = sq.astype(BF16)
    sq_lo = (sq - sq_hi.astype(F32)).astype(BF16)
    ms = _dot(sq_hi, bd_ref[...]) + _dot(sq_lo, bd_ref[...])
    r = _rope(r * lax.rsqrt(ms + RMS_EPS) * gc_ref[...], cos_a, sin_a, HEAD_DIM // 4)
    qc_ref[0] = (r[:, :cq] * C_SCALE).astype(BF16)
    kc_ref[0] = r[:, cq:].astype(BF16)

    r = _dot(h, w3_ref[...])
    va_ref[0] = r[:, :LANES].astype(BF16)
    vc_ref[0] = r[:, LANES:].astype(BF16)

    r = _dot(h, w4_ref[...])
    c_q = _rms(r[:, :B_Q_LORA], gq_ref[...]).astype(BF16)
    c_kv = _rms(r[:, B_Q_LORA:B_Q_LORA + B_KV_LORA], gkv_ref[...]).astype(BF16)
    k_rope = _rope(r[:, B_Q_LORA + B_KV_LORA:], cos_b, sin_b, B_ROPE // 4)
    q = _rope(_dot(c_q, wuq_ref[...]), cos_b, sin_b, B_ROPE // 4)
    qb_ref[0] = (q * B_SCALE).astype(BF16)
    kv = _dot(c_kv, wuk_ref[...])
    kw = B_HEADS * B_HEAD_PAD
    kb_ref[0] = (kv[:, :kw] + jnp.concatenate([k_rope] * B_HEADS, axis=1)).astype(BF16)
    vb_ref[0] = kv[:, kw:].astype(BF16)


def _inproj(x, sc, sh, wts, tabs, tm):
    b, n, d = x.shape
    tm = min(tm, n)
    cos_a, sin_a, cos_b, sin_b = tabs
    full = lambda a: pl.BlockSpec(a.shape, lambda i, bb: (0,) * a.ndim, pipeline_mode=pl.Buffered(1))
    tok = lambda w: pl.BlockSpec((1, tm, w), lambda i, bb: (bb, i, 0))
    mod = pl.BlockSpec((1, 1, d), lambda i, bb: (bb, 0, 0))
    tab = pl.BlockSpec((tm, LANES), lambda i, bb: (i, 0))
    widths = (A_HEADS * HEAD_DIM, LANES, LANES, B_HEADS * B_HEAD_PAD, B_HEADS * B_HEAD_PAD, B_HEADS * B_V,
              C_HEADS * HEAD_DIM, LANES, LANES)
    return pl.pallas_call(
        _inproj_body,
        out_shape=[jax.ShapeDtypeStruct((b, n, w), BF16) for w in widths],
        grid=(n // tm, b),
        in_specs=[tok(d), mod, mod] + [full(a) for a in wts] + [tab] * 4,
        out_specs=[tok(w) for w in widths],
        compiler_params=_params("arbitrary", "arbitrary"),
        name="inproj",
    )(x, sc, sh, *wts, cos_a, sin_a, cos_b, sin_b)


def _softmax_pv(scores, values, sink):
    m = scores[0].max(axis=-1, keepdims=True)
    for s in scores[1:]:
        m = jnp.maximum(m, s.max(axis=-1, keepdims=True))
    if sink is not None:
        m = jnp.maximum(m, sink)
    den = jnp.exp(sink - m) if sink is not None else 0.0
    out = 0.0
    for s, v in zip(scores, values):
        e = jnp.exp(s - m)
        den = den + e.sum(axis=-1, keepdims=True)
        out = out + _dot(e.astype(BF16), v)
    return out / den


def _attn_body(*refs, n_seg, hq, hkv, dq, dv, use_sink):
    q_ref = refs[0]
    kv_refs = refs[1:1 + 2 * n_seg]
    sink_ref = refs[1 + 2 * n_seg] if use_sink else None
    o_ref = refs[-1]
    q = q_ref[0]
    rep = hq // hkv
    outs = []
    for g in range(hkv):
        ks = [kv_refs[2 * s][0][:, g * dq:(g + 1) * dq] for s in range(n_seg)]
        vs = [kv_refs[2 * s + 1][0][:, g * dv:(g + 1) * dv] for s in range(n_seg)]
        for r in range(rep):
            hd = g * rep + r
            qh = q[:, hd * dq:(hd + 1) * dq]
            scores = [_dot_nt(qh, k) for k in ks]
            outs.append(_softmax_pv(scores, vs, sink_ref[hd] if use_sink else None))
    o_ref[0] = jnp.concatenate(outs, axis=1).astype(BF16)


def _attn(q, segs, sink, *, hq, hkv, dq, dv, tq, name):
    b, n, _ = q.shape
    tq = min(tq, n)
    in_specs = [pl.BlockSpec((1, tq, hq * dq), lambda bb, i: (bb, i, 0))]
    args = [q]
    for k, v in segs:
        in_specs.append(pl.BlockSpec((1,) + k.shape[1:], lambda bb, i: (bb, 0, 0)))
        in_specs.append(pl.BlockSpec((1,) + v.shape[1:], lambda bb, i: (bb, 0, 0)))
        args += [k, v]
    if sink is not None:
        in_specs.append(pl.BlockSpec(memory_space=pltpu.SMEM))
        args.append(sink)
    body = functools.partial(_attn_body, n_seg=len(segs), hq=hq, hkv=hkv, dq=dq, dv=dv, use_sink=sink is not None)
    return pl.pallas_call(
        body,
        out_shape=jax.ShapeDtypeStruct((b, n, hq * dv), BF16),
        grid=(b, n // tq),
        in_specs=in_specs,
        out_specs=pl.BlockSpec((1, tq, hq * dv), lambda bb, i: (bb, i, 0)),
        compiler_params=_params("arbitrary", "arbitrary"),
        name=name,
    )(*args)


def _attn_a_body(q_ref, k_ref, v_ref, ck_ref, cv_ref, sink_ref, o_ref, *, seq):
    i = pl.program_id(1)
    start = pl.multiple_of(i * Q_BLOCK, Q_BLOCK)
    span = Q_BLOCK + 2 * WINDOW
    q = q_ref[0]
    k_win = k_ref[0, pl.ds(start, span), :]
    v_win = v_ref[0, pl.ds(start, span), :]
    ck, cv = ck_ref[0], cv_ref[0]
    qpos = start + lax.broadcasted_iota(jnp.int32, (Q_BLOCK, span), 0)
    kpos = start - WINDOW + lax.broadcasted_iota(jnp.int32, (Q_BLOCK, span), 1)
    valid = (kpos >= 0) & (kpos < seq) & (jnp.abs(qpos - kpos) <= WINDOW)
    rep = A_HEADS // A_KV_HEADS
    valid = jnp.concatenate([valid] * rep, axis=0)
    outs = []
    for g in range(A_KV_HEADS):
        sl = slice(g * HEAD_DIM, (g + 1) * HEAD_DIM)
        heads = range(g * rep, (g + 1) * rep)
        qg = jnp.concatenate([q[:, hd * HEAD_DIM:(hd + 1) * HEAD_DIM] for hd in heads], axis=0)
        sink = jnp.concatenate([jnp.full((Q_BLOCK, 1), sink_ref[hd], F32) for hd in heads], axis=0)
        s_loc = jnp.where(valid, _dot_nt(qg, k_win[:, sl]), NEG_BIG)
        s_ctx = _dot_nt(qg, ck[:, sl])
        og = _softmax_pv([s_loc, s_ctx], [v_win[:, sl], cv[:, sl]], sink)
        outs += [og[r * Q_BLOCK:(r + 1) * Q_BLOCK] for r in range(rep)]
    o_ref[0] = jnp.concatenate(outs, axis=1).astype(BF16)


def _attn_a(q, k, v, ck, cv, sink):
    b, n, w = q.shape
    kp = jnp.pad(k, ((0, 0), (WINDOW, WINDOW), (0, 0)))
    vp = jnp.pad(v, ((0, 0), (WINDOW, WINDOW), (0, 0)))
    whole = lambda a: pl.BlockSpec((1,) + a.shape[1:], lambda bb, i: (bb, 0, 0))
    return pl.pallas_call(
        functools.partial(_attn_a_body, seq=n),
        out_shape=jax.ShapeDtypeStruct((b, n, w), BF16),
        grid=(b, n // Q_BLOCK),
        in_specs=[pl.BlockSpec((1, Q_BLOCK, w), lambda bb, i: (bb, i, 0)), whole(kp), whole(vp), whole(ck), whole(cv),
                  pl.BlockSpec(memory_space=pltpu.SMEM)],
        out_specs=pl.BlockSpec((1, Q_BLOCK, w), lambda bb, i: (bb, i, 0)),
        compiler_params=_params("arbitrary", "arbitrary"),
        name="attn_a_window",
    )(q, kp, vp, ck, cv, sink)


def _sigmoid(z):
    return 1.0 / (1.0 + jnp.exp(-z))


def _layer_norm(z, g, b):
    mu = jnp.mean(z, axis=-1, keepdims=True)
    zc = z - mu
    var = jnp.mean(zc * zc, axis=-1, keepdims=True)
    return zc * lax.rsqrt(var + LN_EPS) * g + b


def _merge_body(oa_ref, ob_ref, oc_ref, x_ref, sc1_ref, sh1_ref, g1_ref, sc2_ref, sh2_ref, lng_ref, lnb_ref,
                wg_ref, wa_ref, wb_ref, wc_ref, wo_ref, wr_ref, x1_ref, h2_ref, aff_ref, *, alpha):
    d = x_ref.shape[-1]
    x = x_ref[0]
    h1 = (x * (1.0 + sc1_ref[0]) + sh1_ref[0]).astype(BF16)
    merged = 0.0
    for j, (o_ref, w_ref) in enumerate(((oa_ref, wa_ref), (ob_ref, wb_ref), (oc_ref, wc_ref))):
        gate = _sigmoid(_dot(h1, wg_ref[:, j * d:(j + 1) * d]))
        merged = merged + gate * _dot(o_ref[0], w_ref[...])
    y = _dot(merged.astype(BF16), wo_ref[...])
    x1 = _layer_norm(alpha * x + g1_ref[0] * y, lng_ref[...], lnb_ref[...])
    x1_ref[0] = x1
    h2 = (x1 * (1.0 + sc2_ref[0]) + sh2_ref[0]).astype(BF16)
    h2_ref[0] = h2
    logits = _dot(h2, wr_ref[...])
    lane = lax.broadcasted_iota(jnp.int32, logits.shape, 1)
    logits = jnp.where(lane < N_EXPERTS, logits, NEG_BIG)
    e = jnp.exp(logits - logits.max(axis=-1, keepdims=True))
    aff_ref[0] = e / e.sum(axis=-1, keepdims=True)


def _merge(oa, ob, oc, x, sc1, sh1, g1, sc2, sh2, lng, lnb, wg, wa, wb, wc, wo, wr, alpha, tm):
    b, n, d = x.shape
    tm = min(tm, n)
    tok = lambda w: pl.BlockSpec((1, tm, w), lambda bb, i: (bb, i, 0))
    mod = pl.BlockSpec((1, 1, d), lambda bb, i: (bb, 0, 0))
    full = lambda a: pl.BlockSpec(a.shape, lambda bb, i: (0,) * a.ndim, pipeline_mode=pl.Buffered(1))
    return pl.pallas_call(
        functools.partial(_merge_body, alpha=alpha),
        out_shape=[jax.ShapeDtypeStruct((b, n, d), F32), jax.ShapeDtypeStruct((b, n, d), BF16),
                   jax.ShapeDtypeStruct((b, n, LANES), F32)],
        grid=(b, n // tm),
        in_specs=[tok(oa.shape[2]), tok(ob.shape[2]), tok(oc.shape[2]), tok(d), mod, mod, mod, mod, mod,
                  full(lng), full(lnb), full(wg), full(wa), full(wb), full(wc), full(wo), full(wr)],
        out_specs=[tok(d), tok(d), tok(LANES)],
        compiler_params=_params("arbitrary", "arbitrary"),
        name="merge",
    )(oa, ob, oc, x, sc1, sh1, g1, sc2, sh2, lng, lnb, wg, wa, wb, wc, wo, wr)


def _route_body(aff_ref, tri_ref, slot_ref, *, cap):
    bits = lax.bitcast_convert_type(aff_ref[...], jnp.int32)
    count = lambda m: jnp.sum(jnp.where(m, 1.0, 0.0), axis=1, keepdims=True)
    thr = jnp.zeros((bits.shape[0], 1), jnp.int32)
    for bit in range(30, -1, -1):
        cand = thr | (1 << bit)
        thr = jnp.where(count(bits >= cand) >= cap, cand, thr)
    above = bits > thr
    tie = bits == thr
    need = cap - count(above)
    tri = tri_ref[...]
    tie_rank = _dot(jnp.where(tie, 1.0, 0.0).astype(BF16), tri)
    sel = above | (tie & (tie_rank < need))
    pos = _dot(jnp.where(sel, 1.0, 0.0).astype(BF16), tri)
    slot_ref[...] = jnp.where(sel, pos.astype(jnp.int32), -1)


def _route(aff_t, cap):
    rows, n = aff_t.shape
    tri = jnp.triu(jnp.ones((n, n), BF16), k=1)
    return pl.pallas_call(
        functools.partial(_route_body, cap=cap),
        out_shape=jax.ShapeDtypeStruct((rows, n), jnp.int32),
        compiler_params=pltpu.CompilerParams(vmem_limit_bytes=VMEM_LIMIT),
        name="route",
    )(aff_t, tri)


def _experts_body(h_ref, slot_row_ref, aff_row_ref, slot_tok_ref, wg_ref, wu_ref, wd_ref, o_ref, *, cap):
    e = pl.program_id(1)
    group, n, _ = h_ref.shape
    xs, aff = [], []
    for s in range(group):
        pick = slot_row_ref[s, 0] == lax.broadcasted_iota(jnp.int32, (cap, n), 0)
        xs.append(_dot(jnp.where(pick, 1.0, 0.0).astype(BF16), h_ref[s]).astype(BF16))
        aff.append(jnp.sum(jnp.where(pick, aff_row_ref[s, 0], 0.0), axis=1, keepdims=True))
    xs = xs[0] if group == 1 else jnp.concatenate(xs, axis=0)
    aff = aff[0] if group == 1 else jnp.concatenate(aff, axis=0)
    hid = _dot(xs, wg_ref[0])
    hid = hid * _sigmoid(hid) * _dot(xs, wu_ref[0])
    ys = (_dot(hid.astype(BF16), wd_ref[0]) * aff).astype(BF16)

    @pl.when(e == 0)
    def _():
        o_ref[...] = jnp.zeros_like(o_ref)

    onehot = jnp.where(lax.broadcasted_iota(jnp.int32, (LANES, LANES), 0) == e, 1.0, 0.0).astype(BF16)
    for s in range(group):
        slot_col = _dot(slot_tok_ref[s], onehot)
        parts = []
        for j in range(0, cap, LANES):
            w = min(LANES, cap - j)
            lane = (lax.broadcasted_iota(jnp.int32, (n, w), 1) + j).astype(F32)
            parts.append(jnp.where(slot_col[:, :w] == lane, 1.0, 0.0).astype(BF16))
        put = parts[0] if len(parts) == 1 else jnp.concatenate(parts, axis=1)
        o_ref[s] += _dot(put, ys[s * cap:(s + 1) * cap])


def _experts(h2, slot_t, aff_t, wg, wu, wd, cap):
    b, n, d = h2.shape
    ne, _, f = wg.shape
    group = min(b, max(1, MXU_ROWS // cap))
    while b % group:
        group -= 1
    slot_rows = slot_t.reshape(b, ne, 1, n)
    aff_rows = aff_t.reshape(b, ne, 1, n)
    slot_tok = jnp.transpose(slot_t.reshape(b, ne, n), (0, 2, 1)).astype(BF16)
    slot_tok = jnp.pad(slot_tok, ((0, 0), (0, 0), (0, LANES - ne)))
    row = pl.BlockSpec((group, 1, 1, n), lambda bb, e: (bb, e, 0, 0))
    return pl.pallas_call(
        functools.partial(_experts_body, cap=cap),
        out_shape=jax.ShapeDtypeStruct((b, n, d), F32),
        grid=(b // group, ne),
        in_specs=[pl.BlockSpec((group, n, d), lambda bb, e: (bb, 0, 0)), row, row,
                  pl.BlockSpec((group, n, LANES), lambda bb, e: (bb, 0, 0)),
                  pl.BlockSpec((1, d, f), lambda bb, e: (e, 0, 0)),
                  pl.BlockSpec((1, d, f), lambda bb, e: (e, 0, 0)),
                  pl.BlockSpec((1, f, d), lambda bb, e: (e, 0, 0))],
        out_specs=pl.BlockSpec((group, n, d), lambda bb, e: (bb, 0, 0)),
        compiler_params=_params("arbitrary", "arbitrary"),
        name="experts",
    )(h2, slot_rows, aff_rows, slot_tok, wg, wu, wd)


def _moe(h2, aff, wg, wu, wd):
    b, n, _ = h2.shape
    cap = CAPACITY_FACTOR * n // N_EXPERTS
    aff_t = jnp.transpose(aff[:, :, :N_EXPERTS], (0, 2, 1)).reshape(b * N_EXPERTS, n)
    slot_t = _route(aff_t, cap)
    return _experts(h2, slot_t, aff_t, wg, wu, wd, cap)


def _ln2_body(x_ref, y_ref, g2_ref, lng_ref, lnb_ref, o_ref, *, alpha):
    o_ref[0] = _layer_norm(alpha * x_ref[0] + g2_ref[0] * y_ref[0], lng_ref[...], lnb_ref[...])


def _ln2(x1, y, g2, lng, lnb, alpha, tm):
    b, n, d = x1.shape
    tm = min(tm, n)
    tok = pl.BlockSpec((1, tm, d), lambda bb, i: (bb, i, 0))
    vec = pl.BlockSpec((1, d), lambda bb, i: (0, 0))
    return pl.pallas_call(
        functools.partial(_ln2_body, alpha=alpha),
        out_shape=jax.ShapeDtypeStruct((b, n, d), F32),
        grid=(b, n // tm),
        in_specs=[tok, tok, pl.BlockSpec((1, 1, d), lambda bb, i: (bb, 0, 0)), vec, vec],
        out_specs=tok,
        compiler_params=_params("arbitrary", "arbitrary"),
        name="ln2",
    )(x1, y, g2, lng, lnb)


def _rope_tables(n, enabled):
    lane = np.arange(LANES)
    if not enabled:
        one, zero = jnp.ones((n, LANES), F32), jnp.zeros((n, LANES), F32)
        return one, zero, one, zero
    t = jnp.arange(n)
    pos = jnp.stack([t // GRID_W, t % GRID_W], axis=1).astype(F32)

    def build(jj, d, active):
        freqs = ROPE_THETA ** (-jnp.arange(d, dtype=F32) / d)
        ang = pos[:, (jj // (2 * d)) % 2] * freqs[jj % d][None, :]
        sign = np.where((jj // d) % 2 == 0, -1.0, 1.0).astype(np.float32)
        cos = jnp.where(active[None, :], jnp.cos(ang), 1.0)
        sin = jnp.where(active[None, :], jnp.sin(ang) * sign[None, :], 0.0)
        return cos, sin

    cos_a, sin_a = build(lane % HEAD_DIM, HEAD_DIM // 4, np.ones(LANES, bool))
    in_rope = (lane >= B_NOPE) & (lane < B_NOPE + B_ROPE)
    cos_b, sin_b = build(np.where(in_rope, lane - B_NOPE, 0), B_ROPE // 4, in_rope)
    return cos_a, sin_a, cos_b, sin_b


def _layer_weights(w_in, b_q_norm, b_kv_norm, b_w_uq, b_w_ukv, c_q_norm, c_k_norm):
    d = w_in.shape[0]
    sizes = (A_HEADS * HEAD_DIM, A_KV_HEADS * HEAD_DIM, A_KV_HEADS * HEAD_DIM, B_Q_LORA, B_KV_LORA, B_ROPE,
             C_HEADS * HEAD_DIM, C_KV_HEADS * HEAD_DIM, C_KV_HEADS * HEAD_DIM)
    offs = np.cumsum((0,) + sizes)
    a_q, a_k, a_v, b_cq, b_ckv, b_kr, c_q, c_k, c_v = [w_in[:, offs[i]:offs[i + 1]] for i in range(9)]
    gate = w_in[:, offs[9]:]
    zeros = lambda w: jnp.zeros((d, w), w_in.dtype)
    w1 = jnp.concatenate([a_q, a_k], axis=1)
    w2 = jnp.concatenate([c_q, c_k], axis=1)
    w3 = jnp.concatenate([a_v, c_v], axis=1)
    w4 = jnp.concatenate([b_cq, b_ckv, zeros(B_NOPE), b_kr, zeros(B_HEAD_PAD - B_NOPE - B_ROPE)], axis=1)
    uq = b_w_uq.reshape(B_Q_LORA, B_HEADS, B_NOPE + B_ROPE)
    uq = jnp.pad(uq, ((0, 0), (0, 0), (0, B_HEAD_PAD - B_NOPE - B_ROPE))).reshape(B_Q_LORA, B_HEADS * B_HEAD_PAD)
    ukv = b_w_ukv.reshape(B_KV_LORA, B_HEADS, B_NOPE + B_V)
    uk = jnp.pad(ukv[:, :, :B_NOPE], ((0, 0), (0, 0), (0, B_HEAD_PAD - B_NOPE))).reshape(B_KV_LORA, B_HEADS * B_HEAD_PAD)
    uv = ukv[:, :, B_NOPE:].reshape(B_KV_LORA, B_HEADS * B_V)
    wuk = jnp.concatenate([uk, uv], axis=1)
    width = w2.shape[1]
    blk = np.arange(width) // HEAD_DIM
    bd = jnp.asarray((blk[:, None] == blk[None, :]).astype(np.float32) / HEAD_DIM, BF16)
    gc = jnp.concatenate([jnp.tile(c_q_norm, C_HEADS), jnp.tile(c_k_norm, C_KV_HEADS)])[None, :]
    cast = lambda a: a.astype(BF16)
    return (cast(w1), cast(w2), cast(w3), cast(w4), cast(uq), cast(wuk), bd,
            b_q_norm[None, :], b_kv_norm[None, :], gc), cast(gate)


def kernel(x, c, ctx, c_ctx, w_ada, b_ada, w_in, a_sink, b_q_norm, b_kv_norm, b_w_uq, b_w_ukv, c_q_norm, c_k_norm,
           w_branch, w_out, ln1_g, ln1_b, w_router, w_gate, w_up, w_down, ln2_g, ln2_b):
    bsz, n, d = x.shape
    depth = w_ada.shape[0]
    alpha = (2 * depth) ** 0.25
    aq, bo = A_HEADS * HEAD_DIM, B_HEADS * B_V
    tm = 256

    rows = -(-(bsz + 1) // 8) * 8
    cc = jnp.concatenate([c, c_ctx[None, :], jnp.zeros((rows - bsz - 1, d), c.dtype)], axis=0)
    mod = _ada(cc, w_ada, b_ada)
    tabs_lat = _rope_tables(n, True)
    tabs_ctx = _rope_tables(ctx.shape[1], False)

    b_kw = dict(hq=B_HEADS, hkv=B_HEADS, dq=B_HEAD_PAD, dv=B_V, tq=256)
    c_kw = dict(hq=C_HEADS, hkv=C_KV_HEADS, dq=HEAD_DIM, dv=HEAD_DIM, tq=256)
    a_kw = dict(hq=A_HEADS, hkv=A_KV_HEADS, dq=HEAD_DIM, dv=HEAD_DIM, tq=256)

    cx = ctx
    for l in range(depth):
        last = l == depth - 1
        lat_mod = [m[:, None, :] for m in jnp.split(mod[l, :bsz], 6, axis=-1)]
        ctx_mod = [jnp.broadcast_to(m[None, None, :], (bsz, 1, d)) for m in jnp.split(mod[l, bsz], 6, axis=-1)]
        sh1, sc1, g1, sh2, sc2, g2 = lat_mod
        sh1c, sc1c, g1c, sh2c, sc2c, g2c = ctx_mod
        wts, w_gl = _layer_weights(w_in[l], b_q_norm[l], b_kv_norm[l], b_w_uq[l], b_w_ukv[l], c_q_norm[l], c_k_norm[l])
        wbr = w_branch[l].astype(BF16)
        merge_w = (w_gl, wbr[:aq], wbr[aq:aq + bo], wbr[aq + bo:], w_out[l].astype(BF16),
                   jnp.pad(w_router[l], ((0, 0), (0, LANES - N_EXPERTS))).astype(BF16))
        ffn_w = (w_gate[l].astype(BF16), w_up[l].astype(BF16), w_down[l].astype(BF16))
        ln1 = (ln1_g[l][None, :], ln1_b[l][None, :])
        ln2 = (ln2_g[l][None, :], ln2_b[l][None, :])

        qa, ka, va, qb, kb, vb, qc, kc, vc = _inproj(x, sc1, sh1, wts, tabs_lat, tm)
        cqa, cka, cva, cqb, ckb, cvb, cqc, ckc, cvc = _inproj(cx, sc1c, sh1c, wts, tabs_ctx, tm)

        o_a = _attn_a(qa, ka, va, cka, cva, a_sink[l])
        o_b = _attn(qb, [(ckb, cvb), (kb, vb)], None, name="attn_b", **b_kw)
        o_c = _attn(qc, [(ckc, cvc), (kc, vc)], None, name="attn_c", **c_kw)
        x1, h2, aff = _merge(o_a, o_b, o_c, x, sc1, sh1, g1, sc2, sh2, *ln1, *merge_w, alpha, tm)
        if not last:
            oc_a = _attn(cqa, [(cka, cva)], a_sink[l], name="attn_a_ctx", **a_kw)
            oc_b = _attn(cqb, [(ckb, cvb)], None, name="attn_b_ctx", **b_kw)
            oc_c = _attn(cqc, [(ckc, cvc)], None, name="attn_c_ctx", **c_kw)
            cx1, hc2, affc = _merge(oc_a, oc_b, oc_c, cx, sc1c, sh1c, g1c, sc2c, sh2c, *ln1, *merge_w, alpha, tm)

        x = _ln2(x1, _moe(h2, aff, *ffn_w), g2, *ln2, alpha, tm)
        if not last:
            cx = _ln2(cx1, _moe(hc2, affc, *ffn_w), g2c, *ln2, alpha, tm)
    return x
```

```python
import functools

import jax
import jax.numpy as jnp
import numpy as np
from jax import lax
from jax.experimental import pallas as pl
from jax.experimental.pallas import tpu as pltpu

F32 = jnp.float32
BF16 = jnp.bfloat16

GRID_W = 64
HEAD_DIM = 64
A_HEADS, A_KV_HEADS = 6, 2
WINDOW = 128
Q_BLOCK = 128
B_HEADS, B_NOPE, B_ROPE, B_V = 4, 64, 32, 64
B_Q_LORA, B_KV_LORA = 256, 128
C_HEADS, C_KV_HEADS = 6, 2
N_EXPERTS = 16
CAPACITY_FACTOR = 2
ROPE_THETA = 10000.0
LN_EPS = 1e-5
RMS_EPS = 1e-6
LOG2E = float(np.log2(np.e))
A_SCALE = HEAD_DIM ** -0.5 * LOG2E
B_SCALE = (B_NOPE + B_ROPE) ** -0.5 * LOG2E
C_SCALE = HEAD_DIM ** -0.5 * LOG2E

LANES = 128
BF16_SUBLANES = 16
MXU_ROWS = 256
B_HEAD_PAD = 128
NEG_BIG = -1e30
VMEM_LIMIT = 56 * 1024 * 1024


def _params(*sem):
    return pltpu.CompilerParams(dimension_semantics=sem, vmem_limit_bytes=VMEM_LIMIT)


def _dot(a, b):
    return jnp.dot(a, b, preferred_element_type=F32)


def _dot_nt(a, b):
    return lax.dot_general(a, b, (((1,), (1,)), ((), ())), preferred_element_type=F32)


def _ada_body(c_ref, w_ref, b_ref, o_ref):
    c = c_ref[...]
    s = (c * (1.0 / (1.0 + jnp.exp(-c)))).astype(BF16)
    o_ref[0] = _dot(s, w_ref[0].astype(BF16)) + b_ref[0]


def _ada(cc, w_ada, b_ada):
    depth, d, six_d = w_ada.shape
    rows = cc.shape[0]
    tn = six_d // 4
    return pl.pallas_call(
        _ada_body,
        out_shape=jax.ShapeDtypeStruct((depth, rows, six_d), F32),
        grid=(depth, six_d // tn),
        in_specs=[
            pl.BlockSpec((rows, d), lambda l, j: (0, 0)),
            pl.BlockSpec((1, d, tn), lambda l, j: (l, 0, j)),
            pl.BlockSpec((1, 1, tn), lambda l, j: (l, 0, j)),
        ],
        out_specs=pl.BlockSpec((1, rows, tn), lambda l, j: (l, 0, j)),
        compiler_params=_params("arbitrary", "arbitrary"),
        name="ada",
    )(cc, w_ada, b_ada.reshape(depth, 1, six_d))


def _rope(x, cos, sin, d):
    lane = lax.broadcasted_iota(jnp.int32, (x.shape[0], LANES), 1)
    first = (lane // d) % 2 == 0
    outs = []
    for j in range(x.shape[1] // LANES):
        xs = x[:, j * LANES:(j + 1) * LANES]
        partner = jnp.where(first, pltpu.roll(xs, LANES - d, 1), pltpu.roll(xs, d, 1))
        outs.append(xs * cos + partner * sin)
    return outs[0] if len(outs) == 1 else jnp.concatenate(outs, axis=1)


def _rms(x, gain):
    return x * lax.rsqrt(jnp.mean(x * x, axis=-1, keepdims=True) + RMS_EPS) * gain


def _inproj_body(x_ref, sc_ref, sh_ref, w1_ref, w2_ref, w3_ref, w4_ref, wuq_ref, wuk_ref, bd_ref,
                 gq_ref, gkv_ref, gc_ref, cos_a_ref, sin_a_ref, cos_b_ref, sin_b_ref,
                 qa_ref, ka_ref, va_ref, qb_ref, kb_ref, vb_ref, qc_ref, kc_ref, vc_ref):
    h = (x_ref[0] * (1.0 + sc_ref[0]) + sh_ref[0]).astype(BF16)
    cos_a, sin_a = cos_a_ref[...], sin_a_ref[...]
    cos_b, sin_b = cos_b_ref[...], sin_b_ref[...]
    aq = A_HEADS * HEAD_DIM
    cq = C_HEADS * HEAD_DIM

    r = _rope(_dot(h, w1_ref[...]), cos_a, sin_a, HEAD_DIM // 4)
    qa_ref[0] = (r[:, :aq] * A_SCALE).astype(BF16)
    ka_ref[0] = r[:, aq:].astype(BF16)

    r = _dot(h, w2_ref[...])
    sq = r * r
    sq_hi = sq.astype(BF16)
    sq_lo = (sq - sq_hi.astype(F32)).astype(BF16)
    ms = _dot(sq_hi, bd_ref[...]) + _dot(sq_lo, bd_ref[...])
    r = _rope(r * lax.rsqrt(ms + RMS_EPS) * gc_ref[...], cos_a, sin_a, HEAD_DIM // 4)
    qc_ref[0] = (r[:, :cq] * C_SCALE).astype(BF16)
    kc_ref[0] = r[:, cq:].astype(BF16)

    r = _dot(h, w3_ref[...]).T
    va_ref[0] = r[:LANES].astype(BF16)
    vc_ref[0] = r[LANES:].astype(BF16)

    r = _dot(h, w4_ref[...])
    c_q = _rms(r[:, :B_Q_LORA], gq_ref[...]).astype(BF16)
    c_kv = _rms(r[:, B_Q_LORA:B_Q_LORA + B_KV_LORA], gkv_ref[...]).astype(BF16)
    k_rope = _rope(r[:, B_Q_LORA + B_KV_LORA:], cos_b, sin_b, B_ROPE // 4)
    q = _rope(_dot(c_q, wuq_ref[...]), cos_b, sin_b, B_ROPE // 4)
    qb_ref[0] = (q * B_SCALE).astype(BF16)
    kv = _dot(c_kv, wuk_ref[...])
    kw = B_HEADS * B_HEAD_PAD
    kb_ref[0] = (kv[:, :kw] + jnp.concatenate([k_rope] * B_HEADS, axis=1)).astype(BF16)
    vb_ref[0] = kv[:, kw:].T.astype(BF16)


def _inproj(x, sc, sh, wts, tabs, tm):
    b, n, d = x.shape
    tm = min(tm, n)
    cos_a, sin_a, cos_b, sin_b = tabs
    full = lambda a: pl.BlockSpec(a.shape, lambda i, bb: (0,) * a.ndim, pipeline_mode=pl.Buffered(1))
    tok = lambda w: pl.BlockSpec((1, tm, w), lambda i, bb: (bb, i, 0))
    mod = pl.BlockSpec((1, 1, d), lambda i, bb: (bb, 0, 0))
    tab = pl.BlockSpec((tm, LANES), lambda i, bb: (i, 0))
    widths = (A_HEADS * HEAD_DIM, LANES, -LANES, B_HEADS * B_HEAD_PAD, B_HEADS * B_HEAD_PAD, -B_HEADS * B_V,
              C_HEADS * HEAD_DIM, LANES, -LANES)
    tok_t = lambda w: pl.BlockSpec((1, w, tm), lambda i, bb: (bb, 0, i))
    return pl.pallas_call(
        _inproj_body,
        out_shape=[jax.ShapeDtypeStruct((b, n, w) if w > 0 else (b, -w, n), BF16) for w in widths],
        grid=(n // tm, b),
        in_specs=[tok(d), mod, mod] + [full(a) for a in wts] + [tab] * 4,
        out_specs=[tok(w) if w > 0 else tok_t(-w) for w in widths],
        compiler_params=_params("arbitrary", "arbitrary"),
        name="inproj",
    )(x, sc, sh, *wts, cos_a, sin_a, cos_b, sin_b)


def _attend_t(q, segs, sink):
    scores = []
    for k, _, valid in segs:
        st = _dot_nt(k, q)
        scores.append(st if valid is None else jnp.where(valid, st, NEG_BIG))
    m = scores[0].max(axis=0, keepdims=True)
    for st in scores[1:]:
        m = jnp.maximum(m, st.max(axis=0, keepdims=True))
    if sink is not None:
        m = jnp.maximum(m, sink)
    out_t = 0.0
    for st, (_, vt, _) in zip(scores, segs):
        vt1 = jnp.concatenate([vt, jnp.ones((BF16_SUBLANES, vt.shape[1]), BF16)], axis=0)
        out_t = out_t + _dot(vt1, jnp.exp2(st - m).astype(BF16))
    den = out_t[LANES:LANES + 1]
    if sink is not None:
        den = den + jnp.exp2(sink - m)
    return (out_t[:LANES] / den).T


def _attn_plan(hq, hkv, dq):
    if dq == LANES:
        return [([h], None, h * LANES, (h // 2) * LANES, [h], h % 2, [h // 2]) for h in range(hq)]
    rep = hq // hkv
    assert hkv == 2 and dq == LANES // 2
    return [(list(range(rep)), g, 0, 0, [g * rep + r for r in range(rep)], g, list(range(rep))) for g in range(hkv)]


def _attn_units(q, segs_of, sink_ref, plan):
    tq = q.shape[0]
    low_half = lax.broadcasted_iota(jnp.int32, (tq, LANES), 1) < LANES // 2
    pieces = {}
    for tiles, keep, k0, v0, heads, out_half, out_tiles in plan:
        qs = []
        for t in tiles:
            qt = q[:, t * LANES:(t + 1) * LANES]
            if keep is not None:
                qt = jnp.where(low_half == (keep == 0), qt, jnp.zeros_like(qt))
            qs.append(qt)
        qg = qs[0] if len(qs) == 1 else jnp.concatenate(qs, axis=0)
        sink = None
        if sink_ref is not None:
            sink = jnp.concatenate([jnp.full((1, tq), sink_ref[hd] * LOG2E, F32) for hd in heads], axis=1)
        og = _attend_t(qg, segs_of(k0, v0), sink)
        for j, t in enumerate(out_tiles):
            pieces[(t, out_half)] = og[j * tq:(j + 1) * tq]
    n_out = 1 + max(t for t, _ in pieces)
    outs = [jnp.where(low_half, pieces[(t, 0)], pieces[(t, 1)]) for t in range(n_out)]
    return (outs[0] if n_out == 1 else jnp.concatenate(outs, axis=1)).astype(BF16)


def _attn_body(*refs, n_seg, plan, use_sink):
    q_ref = refs[0]
    kv_refs = refs[1:1 + 2 * n_seg]
    sink_ref = refs[1 + 2 * n_seg] if use_sink else None
    o_ref = refs[-1]

    def segs_of(k0, v0):
        return [(kv_refs[2 * s][0, :, k0:k0 + LANES], kv_refs[2 * s + 1][0, v0:v0 + LANES, :], None)
                for s in range(n_seg)]

    o_ref[0] = _attn_units(q_ref[0], segs_of, sink_ref, plan)


def _attn(q, segs, sink, *, hq, hkv, dq, dv, tq, name):
    b, n, _ = q.shape
    tq = min(tq, n)
    in_specs = [pl.BlockSpec((1, tq, hq * dq), lambda bb, i: (bb, i, 0))]
    args = [q]
    for k, vt in segs:
        in_specs.append(pl.BlockSpec((1,) + k.shape[1:], lambda bb, i: (bb, 0, 0)))
        in_specs.append(pl.BlockSpec((1,) + vt.shape[1:], lambda bb, i: (bb, 0, 0)))
        args += [k, vt]
    if sink is not None:
        in_specs.append(pl.BlockSpec(memory_space=pltpu.SMEM))
        args.append(sink)
    body = functools.partial(_attn_body, n_seg=len(segs), plan=_attn_plan(hq, hkv, dq), use_sink=sink is not None)
    return pl.pallas_call(
        body,
        out_shape=jax.ShapeDtypeStruct((b, n, hq * dv), BF16),
        grid=(b, n // tq),
        in_specs=in_specs,
        out_specs=pl.BlockSpec((1, tq, hq * dv), lambda bb, i: (bb, i, 0)),
        compiler_params=_params("arbitrary", "arbitrary"),
        name=name,
    )(*args)


def _attn_a_body(q_ref, kp_ref, kc_ref, kn_ref, vp_ref, vc_ref, vn_ref, ck_ref, cvt_ref, sink_ref, o_ref, *, seq):
    start = pl.program_id(1) * Q_BLOCK
    span = Q_BLOCK + 2 * WINDOW
    k_win = jnp.concatenate([kp_ref[0], kc_ref[0], kn_ref[0]], axis=0)
    vt_win = jnp.concatenate([vp_ref[0], vc_ref[0], vn_ref[0]], axis=1)
    kpos = start - WINDOW + lax.broadcasted_iota(jnp.int32, (span, Q_BLOCK), 0)
    qpos = start + lax.broadcasted_iota(jnp.int32, (span, Q_BLOCK), 1)
    valid = (kpos >= 0) & (kpos < seq) & (jnp.abs(qpos - kpos) <= WINDOW)
    valid = jnp.concatenate([valid] * (A_HEADS // A_KV_HEADS), axis=1)

    def segs_of(k0, v0):
        return [(k_win, vt_win, valid), (ck_ref[0], cvt_ref[0], None)]

    o_ref[0] = _attn_units(q_ref[0], segs_of, sink_ref, _attn_plan(A_HEADS, A_KV_HEADS, HEAD_DIM))


def _attn_a(q, k, vt, ck, cvt, sink):
    b, n, w = q.shape
    assert WINDOW == Q_BLOCK == LANES
    last = n // Q_BLOCK - 1
    prev = lambda i: jnp.maximum(i - 1, 0)
    nxt = lambda i: jnp.minimum(i + 1, last)
    k_blk = lambda f: pl.BlockSpec((1, Q_BLOCK, LANES), lambda bb, i: (bb, f(i), 0))
    v_blk = lambda f: pl.BlockSpec((1, LANES, Q_BLOCK), lambda bb, i: (bb, 0, f(i)))
    whole = lambda a: pl.BlockSpec((1,) + a.shape[1:], lambda bb, i: (bb, 0, 0))
    same = lambda i: i
    return pl.pallas_call(
        functools.partial(_attn_a_body, seq=n),
        out_shape=jax.ShapeDtypeStruct((b, n, w), BF16),
        grid=(b, n // Q_BLOCK),
        in_specs=[pl.BlockSpec((1, Q_BLOCK, w), lambda bb, i: (bb, i, 0)),
                  k_blk(prev), k_blk(same), k_blk(nxt), v_blk(prev), v_blk(same), v_blk(nxt),
                  whole(ck), whole(cvt), pl.BlockSpec(memory_space=pltpu.SMEM)],
        out_specs=pl.BlockSpec((1, Q_BLOCK, w), lambda bb, i: (bb, i, 0)),
        compiler_params=_params("arbitrary", "arbitrary"),
        name="attn_a_window",
    )(q, k, k, k, vt, vt, vt, ck, cvt, sink)


def _sigmoid(z):
    return 1.0 / (1.0 + jnp.exp(-z))


def _layer_norm(z, g, b):
    mu = jnp.mean(z, axis=-1, keepdims=True)
    zc = z - mu
    var = jnp.mean(zc * zc, axis=-1, keepdims=True)
    return zc * lax.rsqrt(var + LN_EPS) * g + b


def _merge_body(oa_ref, ob_ref, oc_ref, x_ref, sc1_ref, sh1_ref, g1_ref, sc2_ref, sh2_ref, lng_ref, lnb_ref,
                wg_ref, wa_ref, wb_ref, wc_ref, wo_ref, wr_ref, x1_ref, h2_ref, aff_ref, *, alpha):
    d = x_ref.shape[-1]
    x = x_ref[0]
    h1 = (x * (1.0 + sc1_ref[0]) + sh1_ref[0]).astype(BF16)
    merged = 0.0
    for j, (o_ref, w_ref) in enumerate(((oa_ref, wa_ref), (ob_ref, wb_ref), (oc_ref, wc_ref))):
        gate = _sigmoid(_dot(h1, wg_ref[:, j * d:(j + 1) * d]))
        merged = merged + gate * _dot(o_ref[0], w_ref[...])
    y = _dot(merged.astype(BF16), wo_ref[...])
    x1 = _layer_norm(alpha * x + g1_ref[0] * y, lng_ref[...], lnb_ref[...])
    x1_ref[0] = x1
    h2 = (x1 * (1.0 + sc2_ref[0]) + sh2_ref[0]).astype(BF16)
    h2_ref[0] = h2
    logits = _dot(h2, wr_ref[...])
    lane = lax.broadcasted_iota(jnp.int32, logits.shape, 1)
    logits = jnp.where(lane < N_EXPERTS, logits, NEG_BIG)
    e = jnp.exp(logits - logits.max(axis=-1, keepdims=True))
    aff_ref[0] = e / e.sum(axis=-1, keepdims=True)


def _merge(oa, ob, oc, x, sc1, sh1, g1, sc2, sh2, lng, lnb, wg, wa, wb, wc, wo, wr, alpha, tm):
    b, n, d = x.shape
    tm = min(tm, n)
    tok = lambda w: pl.BlockSpec((1, tm, w), lambda bb, i: (bb, i, 0))
    mod = pl.BlockSpec((1, 1, d), lambda bb, i: (bb, 0, 0))
    full = lambda a: pl.BlockSpec(a.shape, lambda bb, i: (0,) * a.ndim, pipeline_mode=pl.Buffered(1))
    return pl.pallas_call(
        functools.partial(_merge_body, alpha=alpha),
        out_shape=[jax.ShapeDtypeStruct((b, n, d), F32), jax.ShapeDtypeStruct((b, n, d), BF16),
                   jax.ShapeDtypeStruct((b, n, LANES), F32)],
        grid=(b, n // tm),
        in_specs=[tok(oa.shape[2]), tok(ob.shape[2]), tok(oc.shape[2]), tok(d), mod, mod, mod, mod, mod,
                  full(lng), full(lnb), full(wg), full(wa), full(wb), full(wc), full(wo), full(wr)],
        out_specs=[tok(d), tok(d), tok(LANES)],
        compiler_params=_params("arbitrary", "arbitrary"),
        name="merge",
    )(oa, ob, oc, x, sc1, sh1, g1, sc2, sh2, lng, lnb, wg, wa, wb, wc, wo, wr)


def _route_body(aff_ref, tri_ref, slot_ref, *, cap):
    bits = lax.bitcast_convert_type(aff_ref[...], jnp.int32)
    count = lambda m: jnp.sum(jnp.where(m, 1.0, 0.0), axis=1, keepdims=True)
    thr = jnp.zeros((bits.shape[0], 1), jnp.int32)
    for bit in range(30, -1, -1):
        cand = thr | (1 << bit)
        thr = jnp.where(count(bits >= cand) >= cap, cand, thr)
    above = bits > thr
    tie = bits == thr
    need = cap - count(above)
    tri = tri_ref[...]
    tie_rank = _dot(jnp.where(tie, 1.0, 0.0).astype(BF16), tri)
    sel = above | (tie & (tie_rank < need))
    pos = _dot(jnp.where(sel, 1.0, 0.0).astype(BF16), tri)
    slot_ref[...] = jnp.where(sel, pos.astype(jnp.int32), -1)


def _route(aff_t, cap):
    rows, n = aff_t.shape
    tri = jnp.triu(jnp.ones((n, n), BF16), k=1)
    return pl.pallas_call(
        functools.partial(_route_body, cap=cap),
        out_shape=jax.ShapeDtypeStruct((rows, n), jnp.int32),
        compiler_params=pltpu.CompilerParams(vmem_limit_bytes=VMEM_LIMIT),
        name="route",
    )(aff_t, tri)


def _experts_body(h_ref, slot_row_ref, aff_row_ref, slot_tok_ref, wg_ref, wu_ref, wd_ref, o_ref, *, cap):
    e = pl.program_id(1)
    group, n, _ = h_ref.shape
    xs, aff = [], []
    for s in range(group):
        pick = slot_row_ref[s, 0] == lax.broadcasted_iota(jnp.int32, (cap, n), 0)
        xs.append(_dot(jnp.where(pick, 1.0, 0.0).astype(BF16), h_ref[s]).astype(BF16))
        aff.append(jnp.sum(jnp.where(pick, aff_row_ref[s, 0], 0.0), axis=1, keepdims=True))
    xs = xs[0] if group == 1 else jnp.concatenate(xs, axis=0)
    aff = aff[0] if group == 1 else jnp.concatenate(aff, axis=0)
    hid = _dot(xs, wg_ref[0])
    hid = hid * _sigmoid(hid) * _dot(xs, wu_ref[0])
    ys = (_dot(hid.astype(BF16), wd_ref[0]) * aff).astype(BF16)

    @pl.when(e == 0)
    def _():
        o_ref[...] = jnp.zeros_like(o_ref)

    onehot = jnp.where(lax.broadcasted_iota(jnp.int32, (LANES, LANES), 0) == e, 1.0, 0.0).astype(BF16)
    for s in range(group):
        slot_col = _dot(slot_tok_ref[s], onehot)
        parts = []
        for j in range(0, cap, LANES):
            w = min(LANES, cap - j)
            lane = (lax.broadcasted_iota(jnp.int32, (n, w), 1) + j).astype(F32)
            parts.append(jnp.where(slot_col[:, :w] == lane, 1.0, 0.0).astype(BF16))
        put = parts[0] if len(parts) == 1 else jnp.concatenate(parts, axis=1)
        o_ref[s] += _dot(put, ys[s * cap:(s + 1) * cap])


def _experts(h2, slot_t, aff_t, wg, wu, wd, cap):
    b, n, d = h2.shape
    ne, _, f = wg.shape
    group = min(b, max(1, MXU_ROWS // cap))
    while b % group:
        group -= 1
    slot_rows = slot_t.reshape(b, ne, 1, n)
    aff_rows = aff_t.reshape(b, ne, 1, n)
    slot_tok = jnp.transpose(slot_t.reshape(b, ne, n), (0, 2, 1)).astype(BF16)
    slot_tok = jnp.pad(slot_tok, ((0, 0), (0, 0), (0, LANES - ne)))
    row = pl.BlockSpec((group, 1, 1, n), lambda bb, e: (bb, e, 0, 0))
    return pl.pallas_call(
        functools.partial(_experts_body, cap=cap),
        out_shape=jax.ShapeDtypeStruct((b, n, d), F32),
        grid=(b // group, ne),
        in_specs=[pl.BlockSpec((group, n, d), lambda bb, e: (bb, 0, 0)), row, row,
                  pl.BlockSpec((group, n, LANES), lambda bb, e: (bb, 0, 0)),
                  pl.BlockSpec((1, d, f), lambda bb, e: (e, 0, 0)),
                  pl.BlockSpec((1, d, f), lambda bb, e: (e, 0, 0)),
                  pl.BlockSpec((1, f, d), lambda bb, e: (e, 0, 0))],
        out_specs=pl.BlockSpec((group, n, d), lambda bb, e: (bb, 0, 0)),
        compiler_params=_params("arbitrary", "arbitrary"),
        name="experts",
    )(h2, slot_rows, aff_rows, slot_tok, wg, wu, wd)


def _moe(h2, aff, wg, wu, wd):
    b, n, _ = h2.shape
    cap = CAPACITY_FACTOR * n // N_EXPERTS
    aff_t = jnp.transpose(aff[:, :, :N_EXPERTS], (0, 2, 1)).reshape(b * N_EXPERTS, n)
    slot_t = _route(aff_t, cap)
    return _experts(h2, slot_t, aff_t, wg, wu, wd, cap)


def _ln2_body(x_ref, y_ref, g2_ref, lng_ref, lnb_ref, o_ref, *, alpha):
    o_ref[0] = _layer_norm(alpha * x_ref[0] + g2_ref[0] * y_ref[0], lng_ref[...], lnb_ref[...])


def _ln2(x1, y, g2, lng, lnb, alpha, tm):
    b, n, d = x1.shape
    tm = min(tm, n)
    tok = pl.BlockSpec((1, tm, d), lambda bb, i: (bb, i, 0))
    vec = pl.BlockSpec((1, d), lambda bb, i: (0, 0))
    return pl.pallas_call(
        functools.partial(_ln2_body, alpha=alpha),
        out_shape=jax.ShapeDtypeStruct((b, n, d), F32),
        grid=(b, n // tm),
        in_specs=[tok, tok, pl.BlockSpec((1, 1, d), lambda bb, i: (bb, 0, 0)), vec, vec],
        out_specs=tok,
        compiler_params=_params("arbitrary", "arbitrary"),
        name="ln2",
    )(x1, y, g2, lng, lnb)


def _rope_tables(n, enabled):
    lane = np.arange(LANES)
    if not enabled:
        one, zero = jnp.ones((n, LANES), F32), jnp.zeros((n, LANES), F32)
        return one, zero, one, zero
    t = jnp.arange(n)
    pos = jnp.stack([t // GRID_W, t % GRID_W], axis=1).astype(F32)

    def build(jj, d, active):
        freqs = ROPE_THETA ** (-jnp.arange(d, dtype=F32) / d)
        ang = pos[:, (jj // (2 * d)) % 2] * freqs[jj % d][None, :]
        sign = np.where((jj // d) % 2 == 0, -1.0, 1.0).astype(np.float32)
        cos = jnp.where(active[None, :], jnp.cos(ang), 1.0)
        sin = jnp.where(active[None, :], jnp.sin(ang) * sign[None, :], 0.0)
        return cos, sin

    cos_a, sin_a = build(lane % HEAD_DIM, HEAD_DIM // 4, np.ones(LANES, bool))
    in_rope = (lane >= B_NOPE) & (lane < B_NOPE + B_ROPE)
    cos_b, sin_b = build(np.where(in_rope, lane - B_NOPE, 0), B_ROPE // 4, in_rope)
    return cos_a, sin_a, cos_b, sin_b


def _pair_heads(w, axis, heads, groups):
    shape = w.shape
    split = shape[:axis] + (groups, heads // groups, HEAD_DIM) + shape[axis + 1:]
    return jnp.swapaxes(w.reshape(split), axis, axis + 1).reshape(shape)


def _layer_weights(w_in, b_q_norm, b_kv_norm, b_w_uq, b_w_ukv, c_q_norm, c_k_norm):
    d = w_in.shape[0]
    sizes = (A_HEADS * HEAD_DIM, A_KV_HEADS * HEAD_DIM, A_KV_HEADS * HEAD_DIM, B_Q_LORA, B_KV_LORA, B_ROPE,
             C_HEADS * HEAD_DIM, C_KV_HEADS * HEAD_DIM, C_KV_HEADS * HEAD_DIM)
    offs = np.cumsum((0,) + sizes)
    a_q, a_k, a_v, b_cq, b_ckv, b_kr, c_q, c_k, c_v = [w_in[:, offs[i]:offs[i + 1]] for i in range(9)]
    gate = w_in[:, offs[9]:]
    zeros = lambda w: jnp.zeros((d, w), w_in.dtype)
    w1 = jnp.concatenate([_pair_heads(a_q, 1, A_HEADS, A_KV_HEADS), a_k], axis=1)
    w2 = jnp.concatenate([_pair_heads(c_q, 1, C_HEADS, C_KV_HEADS), c_k], axis=1)
    w3 = jnp.concatenate([a_v, c_v], axis=1)
    w4 = jnp.concatenate([b_cq, b_ckv, zeros(B_NOPE), b_kr, zeros(B_HEAD_PAD - B_NOPE - B_ROPE)], axis=1)
    uq = b_w_uq.reshape(B_Q_LORA, B_HEADS, B_NOPE + B_ROPE)
    uq = jnp.pad(uq, ((0, 0), (0, 0), (0, B_HEAD_PAD - B_NOPE - B_ROPE))).reshape(B_Q_LORA, B_HEADS * B_HEAD_PAD)
    ukv = b_w_ukv.reshape(B_KV_LORA, B_HEADS, B_NOPE + B_V)
    uk = jnp.pad(ukv[:, :, :B_NOPE], ((0, 0), (0, 0), (0, B_HEAD_PAD - B_NOPE))).reshape(B_KV_LORA, B_HEADS * B_HEAD_PAD)
    uv = ukv[:, :, B_NOPE:].reshape(B_KV_LORA, B_HEADS * B_V)
    wuk = jnp.concatenate([uk, uv], axis=1)
    width = w2.shape[1]
    blk = np.arange(width) // HEAD_DIM
    bd = jnp.asarray((blk[:, None] == blk[None, :]).astype(np.float32) / HEAD_DIM, BF16)
    gc = jnp.concatenate([jnp.tile(c_q_norm, C_HEADS), jnp.tile(c_k_norm, C_KV_HEADS)])[None, :]
    cast = lambda a: a.astype(BF16)
    return (cast(w1), cast(w2), cast(w3), cast(w4), cast(uq), cast(wuk), bd,
            b_q_norm[None, :], b_kv_norm[None, :], gc), cast(gate)


def kernel(x, c, ctx, c_ctx, w_ada, b_ada, w_in, a_sink, b_q_norm, b_kv_norm, b_w_uq, b_w_ukv, c_q_norm, c_k_norm,
           w_branch, w_out, ln1_g, ln1_b, w_router, w_gate, w_up, w_down, ln2_g, ln2_b):
    bsz, n, d = x.shape
    depth = w_ada.shape[0]
    alpha = (2 * depth) ** 0.25
    aq, bo = A_HEADS * HEAD_DIM, B_HEADS * B_V
    tm = 512

    rows = -(-(bsz + 1) // 8) * 8
    cc = jnp.concatenate([c, c_ctx[None, :], jnp.zeros((rows - bsz - 1, d), c.dtype)], axis=0)
    mod = _ada(cc, w_ada, b_ada)
    tabs_lat = _rope_tables(n, True)
    tabs_ctx = _rope_tables(ctx.shape[1], False)

    b_kw = dict(hq=B_HEADS, hkv=B_HEADS, dq=B_HEAD_PAD, dv=B_V, tq=1024)
    c_kw = dict(hq=C_HEADS, hkv=C_KV_HEADS, dq=HEAD_DIM, dv=HEAD_DIM, tq=512)
    a_kw = dict(hq=A_HEADS, hkv=A_KV_HEADS, dq=HEAD_DIM, dv=HEAD_DIM, tq=256)

    cx = ctx
    for l in range(depth):
        last = l == depth - 1
        lat_mod = [m[:, None, :] for m in jnp.split(mod[l, :bsz], 6, axis=-1)]
        ctx_mod = [jnp.broadcast_to(m[None, None, :], (bsz, 1, d)) for m in jnp.split(mod[l, bsz], 6, axis=-1)]
        sh1, sc1, g1, sh2, sc2, g2 = lat_mod
        sh1c, sc1c, g1c, sh2c, sc2c, g2c = ctx_mod
        wts, w_gl = _layer_weights(w_in[l], b_q_norm[l], b_kv_norm[l], b_w_uq[l], b_w_ukv[l], c_q_norm[l], c_k_norm[l])
        wbr = w_branch[l].astype(BF16)
        merge_w = (w_gl, _pair_heads(wbr[:aq], 0, A_HEADS, A_KV_HEADS), wbr[aq:aq + bo],
                   _pair_heads(wbr[aq + bo:], 0, C_HEADS, C_KV_HEADS), w_out[l].astype(BF16),
                   jnp.pad(w_router[l], ((0, 0), (0, LANES - N_EXPERTS))).astype(BF16))
        ffn_w = (w_gate[l].astype(BF16), w_up[l].astype(BF16), w_down[l].astype(BF16))
        ln1 = (ln1_g[l][None, :], ln1_b[l][None, :])
        ln2 = (ln2_g[l][None, :], ln2_b[l][None, :])

        qa, ka, va, qb, kb, vb, qc, kc, vc = _inproj(x, sc1, sh1, wts, tabs_lat, tm)
        cqa, cka, cva, cqb, ckb, cvb, cqc, ckc, cvc = _inproj(cx, sc1c, sh1c, wts, tabs_ctx, tm)

        o_a = _attn_a(qa, ka, va, cka, cva, a_sink[l])
        o_b = _attn(qb, [(ckb, cvb), (kb, vb)], None, name="attn_b", **b_kw)
        o_c = _attn(qc, [(ckc, cvc), (kc, vc)], None, name="attn_c", **c_kw)
        x1, h2, aff = _merge(o_a, o_b, o_c, x, sc1, sh1, g1, sc2, sh2, *ln1, *merge_w, alpha, tm)
        if not last:
            oc_a = _attn(cqa, [(cka, cva)], a_sink[l], name="attn_a_ctx", **a_kw)
            oc_b = _attn(cqb, [(ckb, cvb)], None, name="attn_b_ctx", **b_kw)
            oc_c = _attn(cqc, [(ckc, cvc)], None, name="attn_c_ctx", **c_kw)
            cx1, hc2, affc = _merge(oc_a, oc_b, oc_c, cx, sc1c, sh1c, g1c, sc2c, sh2c, *ln1, *merge_w, alpha, tm)

        x = _ln2(x1, _moe(h2, aff, *ffn_w), g2, *ln2, alpha, tm)
        if not last:
            cx = _ln2(cx1, _moe(hc2, affc, *ffn_w), g2c, *ln2, alpha, tm)
    return x
```

```python
import functools

import jax
import jax.numpy as jnp
import numpy as np
from jax import lax
from jax.experimental import pallas as pl
from jax.experimental.pallas import tpu as pltpu

F32 = jnp.float32
BF16 = jnp.bfloat16

GRID_W = 64
HEAD_DIM = 64
A_HEADS, A_KV_HEADS = 6, 2
WINDOW = 128
B_HEADS, B_NOPE, B_ROPE, B_V = 4, 64, 32, 64
B_Q_LORA, B_KV_LORA = 256, 128
C_HEADS, C_KV_HEADS = 6, 2
N_EXPERTS = 16
CAPACITY_FACTOR = 2
ROPE_THETA = 10000.0
LN_EPS = 1e-5
RMS_EPS = 1e-6
LOG2E = float(np.log2(np.e))
A_SCALE = HEAD_DIM ** -0.5 * LOG2E
B_SCALE = (B_NOPE + B_ROPE) ** -0.5 * LOG2E
C_SCALE = HEAD_DIM ** -0.5 * LOG2E

LANES = 128
BF16_SUBLANES = 16
MXU_ROWS = 256
B_HEAD_PAD = 128
NEG_BIG = -1e30
VMEM_LIMIT = 56 * 1024 * 1024


def _params(*sem):
    return pltpu.CompilerParams(dimension_semantics=sem, vmem_limit_bytes=VMEM_LIMIT)


def _dot(a, b):
    return jnp.dot(a, b, preferred_element_type=F32)


def _dot_nt(a, b):
    return lax.dot_general(a, b, (((1,), (1,)), ((), ())), preferred_element_type=F32)


def _ada_body(c_ref, w_ref, b_ref, o_ref):
    c = c_ref[...]
    s = (c * (1.0 / (1.0 + jnp.exp(-c)))).astype(BF16)
    o_ref[0] = _dot(s, w_ref[0].astype(BF16)) + b_ref[0]


def _ada(cc, w_ada, b_ada):
    depth, d, six_d = w_ada.shape
    rows = cc.shape[0]
    tn = six_d // 4
    return pl.pallas_call(
        _ada_body,
        out_shape=jax.ShapeDtypeStruct((depth, rows, six_d), F32),
        grid=(depth, six_d // tn),
        in_specs=[
            pl.BlockSpec((rows, d), lambda l, j: (0, 0)),
            pl.BlockSpec((1, d, tn), lambda l, j: (l, 0, j)),
            pl.BlockSpec((1, 1, tn), lambda l, j: (l, 0, j)),
        ],
        out_specs=pl.BlockSpec((1, rows, tn), lambda l, j: (l, 0, j)),
        compiler_params=_params("arbitrary", "arbitrary"),
        name="ada",
    )(cc, w_ada, b_ada.reshape(depth, 1, six_d))


def _rope(x, cos, sin, d):
    lane = lax.broadcasted_iota(jnp.int32, (x.shape[0], LANES), 1)
    first = (lane // d) % 2 == 0
    outs = []
    for j in range(x.shape[1] // LANES):
        xs = x[:, j * LANES:(j + 1) * LANES]
        partner = jnp.where(first, pltpu.roll(xs, LANES - d, 1), pltpu.roll(xs, d, 1))
        outs.append(xs * cos + partner * sin)
    return outs[0] if len(outs) == 1 else jnp.concatenate(outs, axis=1)


def _rms(x, gain):
    return x * lax.rsqrt(jnp.mean(x * x, axis=-1, keepdims=True) + RMS_EPS) * gain


def _inproj_body(x_ref, sc_ref, sh_ref, w1_ref, w2_ref, w3_ref, w4_ref, wuq_ref, wuk_ref, bd_ref,
                 gq_ref, gkv_ref, gc_ref, cos_a_ref, sin_a_ref, cos_b_ref, sin_b_ref,
                 qa_ref, ka_ref, va_ref, qb_ref, kb_ref, vb_ref, qc_ref, kc_ref, vc_ref):
    h = (x_ref[0] * (1.0 + sc_ref[0]) + sh_ref[0]).astype(BF16)
    cos_a, sin_a = cos_a_ref[...], sin_a_ref[...]
    cos_b, sin_b = cos_b_ref[...], sin_b_ref[...]
    aq = A_HEADS * HEAD_DIM
    cq = C_HEADS * HEAD_DIM

    r = _rope(_dot(h, w1_ref[...]), cos_a, sin_a, HEAD_DIM // 4)
    qa_ref[0] = (r[:, :aq] * A_SCALE).astype(BF16)
    ka_ref[0] = r[:, aq:].astype(BF16)

    r = _dot(h, w2_ref[...])
    sq = r * r
    sq_hi = sq.astype(BF16)
    sq_lo = (sq - sq_hi.astype(F32)).astype(BF16)
    ms = _dot(sq_hi, bd_ref[...]) + _dot(sq_lo, bd_ref[...])
    r = _rope(r * lax.rsqrt(ms + RMS_EPS) * gc_ref[...], cos_a, sin_a, HEAD_DIM // 4)
    qc_ref[0] = (r[:, :cq] * C_SCALE).astype(BF16)
    kc_ref[0] = r[:, cq:].astype(BF16)

    r = _dot(h, w3_ref[...]).T
    va_ref[0] = r[:LANES].astype(BF16)
    vc_ref[0] = r[LANES:].astype(BF16)

    r = _dot(h, w4_ref[...])
    c_q = _rms(r[:, :B_Q_LORA], gq_ref[...]).astype(BF16)
    c_kv = _rms(r[:, B_Q_LORA:B_Q_LORA + B_KV_LORA], gkv_ref[...]).astype(BF16)
    k_rope = _rope(r[:, B_Q_LORA + B_KV_LORA:], cos_b, sin_b, B_ROPE // 4)
    q = _rope(_dot(c_q, wuq_ref[...]), cos_b, sin_b, B_ROPE // 4)
    qb_ref[0] = (q * B_SCALE).astype(BF16)
    kv = _dot(c_kv, wuk_ref[...])
    kw = B_HEADS * B_HEAD_PAD
    kb_ref[0] = (kv[:, :kw] + jnp.concatenate([k_rope] * B_HEADS, axis=1)).astype(BF16)
    vb_ref[0] = kv[:, kw:].T.astype(BF16)


def _inproj(x, sc, sh, wts, tabs, tm):
    b, n, d = x.shape
    tm = min(tm, n)
    cos_a, sin_a, cos_b, sin_b = tabs
    full = lambda a: pl.BlockSpec(a.shape, lambda i, bb: (0,) * a.ndim, pipeline_mode=pl.Buffered(1))
    tok = lambda w: pl.BlockSpec((1, tm, w), lambda i, bb: (bb, i, 0))
    mod = pl.BlockSpec((1, 1, d), lambda i, bb: (bb, 0, 0))
    tab = pl.BlockSpec((tm, LANES), lambda i, bb: (i, 0))
    widths = (A_HEADS * HEAD_DIM, LANES, -LANES, B_HEADS * B_HEAD_PAD, B_HEADS * B_HEAD_PAD, -B_HEADS * B_V,
              C_HEADS * HEAD_DIM, LANES, -LANES)
    tok_t = lambda w: pl.BlockSpec((1, w, tm), lambda i, bb: (bb, 0, i))
    return pl.pallas_call(
        _inproj_body,
        out_shape=[jax.ShapeDtypeStruct((b, n, w) if w > 0 else (b, -w, n), BF16) for w in widths],
        grid=(n // tm, b),
        in_specs=[tok(d), mod, mod] + [full(a) for a in wts] + [tab] * 4,
        out_specs=[tok(w) if w > 0 else tok_t(-w) for w in widths],
        compiler_params=_params("arbitrary", "arbitrary"),
        name="inproj",
    )(x, sc, sh, *wts, cos_a, sin_a, cos_b, sin_b)


def _attend_t(q, segs, sink):
    scores = []
    for k, _, valid in segs:
        st = _dot_nt(k, q)
        scores.append(st if valid is None else jnp.where(valid, st, NEG_BIG))
    m = scores[0].max(axis=0, keepdims=True)
    for st in scores[1:]:
        m = jnp.maximum(m, st.max(axis=0, keepdims=True))
    if sink is not None:
        m = jnp.maximum(m, sink)
    out_t = 0.0
    for st, (_, vt, _) in zip(scores, segs):
        vt1 = jnp.concatenate([vt, jnp.ones((BF16_SUBLANES, vt.shape[1]), BF16)], axis=0)
        out_t = out_t + _dot(vt1, jnp.exp2(st - m).astype(BF16))
    den = out_t[LANES:LANES + 1]
    if sink is not None:
        den = den + jnp.exp2(sink - m)
    return (out_t[:LANES] / den).T


def _attn_plan(hq, hkv, dq):
    if dq == LANES:
        return [([h], None, h * LANES, (h // 2) * LANES, [h], h % 2, [h // 2]) for h in range(hq)]
    rep = hq // hkv
    assert hkv == 2 and dq == LANES // 2
    return [(list(range(rep)), g, 0, 0, [g * rep + r for r in range(rep)], g, list(range(rep))) for g in range(hkv)]


def _attn_units(q, segs_of, sink_ref, plan):
    tq = q.shape[0]
    low_half = lax.broadcasted_iota(jnp.int32, (tq, LANES), 1) < LANES // 2
    pieces = {}
    for tiles, keep, k0, v0, heads, out_half, out_tiles in plan:
        qs = []
        for t in tiles:
            qt = q[:, t * LANES:(t + 1) * LANES]
            if keep is not None:
                qt = jnp.where(low_half == (keep == 0), qt, jnp.zeros_like(qt))
            qs.append(qt)
        qg = qs[0] if len(qs) == 1 else jnp.concatenate(qs, axis=0)
        sink = None
        if sink_ref is not None:
            sink = jnp.concatenate([jnp.full((1, tq), sink_ref[hd] * LOG2E, F32) for hd in heads], axis=1)
        og = _attend_t(qg, segs_of(k0, v0), sink)
        for j, t in enumerate(out_tiles):
            pieces[(t, out_half)] = og[j * tq:(j + 1) * tq]
    n_out = 1 + max(t for t, _ in pieces)
    outs = [jnp.where(low_half, pieces[(t, 0)], pieces[(t, 1)]) for t in range(n_out)]
    return (outs[0] if n_out == 1 else jnp.concatenate(outs, axis=1)).astype(BF16)


def _attn_body(*refs, n_seg, plan, use_sink):
    q_ref = refs[0]
    kv_refs = refs[1:1 + 2 * n_seg]
    sink_ref = refs[1 + 2 * n_seg] if use_sink else None
    o_ref = refs[-1]

    def segs_of(k0, v0):
        return [(kv_refs[2 * s][0, :, k0:k0 + LANES], kv_refs[2 * s + 1][0, v0:v0 + LANES, :], None)
                for s in range(n_seg)]

    o_ref[0] = _attn_units(q_ref[0], segs_of, sink_ref, plan)


def _attn(q, segs, sink, *, hq, hkv, dq, dv, tq, name):
    b, n, _ = q.shape
    tq = min(tq, n)
    in_specs = [pl.BlockSpec((1, tq, hq * dq), lambda bb, i: (bb, i, 0))]
    args = [q]
    for k, vt in segs:
        in_specs.append(pl.BlockSpec((1,) + k.shape[1:], lambda bb, i: (bb, 0, 0)))
        in_specs.append(pl.BlockSpec((1,) + vt.shape[1:], lambda bb, i: (bb, 0, 0)))
        args += [k, vt]
    if sink is not None:
        in_specs.append(pl.BlockSpec(memory_space=pltpu.SMEM))
        args.append(sink)
    body = functools.partial(_attn_body, n_seg=len(segs), plan=_attn_plan(hq, hkv, dq), use_sink=sink is not None)
    return pl.pallas_call(
        body,
        out_shape=jax.ShapeDtypeStruct((b, n, hq * dv), BF16),
        grid=(b, n // tq),
        in_specs=in_specs,
        out_specs=pl.BlockSpec((1, tq, hq * dv), lambda bb, i: (bb, i, 0)),
        compiler_params=_params("arbitrary", "arbitrary"),
        name=name,
    )(*args)


def _attn_a_body(*refs, seq, tq, nkb):
    q_ref = refs[0]
    k_refs, v_refs = refs[1:1 + nkb], refs[1 + nkb:1 + 2 * nkb]
    ck_ref, cvt_ref, sink_ref, o_ref = refs[1 + 2 * nkb:]
    start = pl.program_id(1) * tq
    span = nkb * WINDOW
    k_win = jnp.concatenate([r[0] for r in k_refs], axis=0)
    vt_win = jnp.concatenate([r[0] for r in v_refs], axis=1)
    kpos = start - WINDOW + lax.broadcasted_iota(jnp.int32, (span, tq), 0)
    qpos = start + lax.broadcasted_iota(jnp.int32, (span, tq), 1)
    valid = (kpos >= 0) & (kpos < seq) & (jnp.abs(qpos - kpos) <= WINDOW)
    valid = jnp.concatenate([valid] * (A_HEADS // A_KV_HEADS), axis=1)

    def segs_of(k0, v0):
        return [(k_win, vt_win, valid), (ck_ref[0], cvt_ref[0], None)]

    o_ref[0] = _attn_units(q_ref[0], segs_of, sink_ref, _attn_plan(A_HEADS, A_KV_HEADS, HEAD_DIM))


def _attn_a(q, k, vt, ck, cvt, sink, tq):
    b, n, w = q.shape
    assert WINDOW == LANES and tq % WINDOW == 0 and n % tq == 0
    per = tq // WINDOW
    nkb = per + 2
    last = n // WINDOW - 1
    blk = lambda j: (lambda i: jnp.clip(i * per + j - 1, 0, last))
    k_blk = lambda f: pl.BlockSpec((1, WINDOW, LANES), lambda bb, i: (bb, f(i), 0))
    v_blk = lambda f: pl.BlockSpec((1, LANES, WINDOW), lambda bb, i: (bb, 0, f(i)))
    whole = lambda a: pl.BlockSpec((1,) + a.shape[1:], lambda bb, i: (bb, 0, 0))
    return pl.pallas_call(
        functools.partial(_attn_a_body, seq=n, tq=tq, nkb=nkb),
        out_shape=jax.ShapeDtypeStruct((b, n, w), BF16),
        grid=(b, n // tq),
        in_specs=[pl.BlockSpec((1, tq, w), lambda bb, i: (bb, i, 0))]
                 + [k_blk(blk(j)) for j in range(nkb)] + [v_blk(blk(j)) for j in range(nkb)]
                 + [whole(ck), whole(cvt), pl.BlockSpec(memory_space=pltpu.SMEM)],
        out_specs=pl.BlockSpec((1, tq, w), lambda bb, i: (bb, i, 0)),
        compiler_params=_params("arbitrary", "arbitrary"),
        name="attn_a_window",
    )(q, *([k] * nkb), *([vt] * nkb), ck, cvt, sink)


def _sigmoid(z):
    return 1.0 / (1.0 + jnp.exp(-z))


def _layer_norm(z, g, b):
    mu = jnp.mean(z, axis=-1, keepdims=True)
    zc = z - mu
    var = jnp.mean(zc * zc, axis=-1, keepdims=True)
    return zc * lax.rsqrt(var + LN_EPS) * g + b


def _merge_body(oa_ref, ob_ref, oc_ref, x_ref, sc1_ref, sh1_ref, g1_ref, sc2_ref, sh2_ref, lng_ref, lnb_ref,
                wg_ref, wa_ref, wb_ref, wc_ref, wo_ref, wr_ref, x1_ref, h2_ref, aff_ref, *, alpha):
    d = x_ref.shape[-1]
    x = x_ref[0]
    h1 = (x * (1.0 + sc1_ref[0]) + sh1_ref[0]).astype(BF16)
    merged = 0.0
    for j, (o_ref, w_ref) in enumerate(((oa_ref, wa_ref), (ob_ref, wb_ref), (oc_ref, wc_ref))):
        gate = _sigmoid(_dot(h1, wg_ref[:, j * d:(j + 1) * d]))
        merged = merged + gate * _dot(o_ref[0], w_ref[...])
    y = _dot(merged.astype(BF16), wo_ref[...])
    x1 = _layer_norm(alpha * x + g1_ref[0] * y, lng_ref[...], lnb_ref[...])
    x1_ref[0] = x1
    h2 = (x1 * (1.0 + sc2_ref[0]) + sh2_ref[0]).astype(BF16)
    h2_ref[0] = h2
    logits = _dot(h2, wr_ref[...])
    lane = lax.broadcasted_iota(jnp.int32, logits.shape, 1)
    logits = jnp.where(lane < N_EXPERTS, logits, NEG_BIG)
    e = jnp.exp(logits - logits.max(axis=-1, keepdims=True))
    aff_ref[0] = e / e.sum(axis=-1, keepdims=True)


def _merge(oa, ob, oc, x, sc1, sh1, g1, sc2, sh2, lng, lnb, wg, wa, wb, wc, wo, wr, alpha, tm):
    b, n, d = x.shape
    tm = min(tm, n)
    tok = lambda w: pl.BlockSpec((1, tm, w), lambda bb, i: (bb, i, 0))
    mod = pl.BlockSpec((1, 1, d), lambda bb, i: (bb, 0, 0))
    full = lambda a: pl.BlockSpec(a.shape, lambda bb, i: (0,) * a.ndim, pipeline_mode=pl.Buffered(1))
    return pl.pallas_call(
        functools.partial(_merge_body, alpha=alpha),
        out_shape=[jax.ShapeDtypeStruct((b, n, d), F32), jax.ShapeDtypeStruct((b, n, d), BF16),
                   jax.ShapeDtypeStruct((b, n, LANES), F32)],
        grid=(b, n // tm),
        in_specs=[tok(oa.shape[2]), tok(ob.shape[2]), tok(oc.shape[2]), tok(d), mod, mod, mod, mod, mod,
                  full(lng), full(lnb), full(wg), full(wa), full(wb), full(wc), full(wo), full(wr)],
        out_specs=[tok(d), tok(d), tok(LANES)],
        compiler_params=_params("arbitrary", "arbitrary"),
        name="merge",
    )(oa, ob, oc, x, sc1, sh1, g1, sc2, sh2, lng, lnb, wg, wa, wb, wc, wo, wr)


def _route_body(aff_ref, tri_ref, slot_ref, *, cap):
    bits = lax.bitcast_convert_type(aff_ref[...], jnp.int32)
    count = lambda m: jnp.sum(jnp.where(m, 1.0, 0.0), axis=1, keepdims=True)
    thr = jnp.zeros((bits.shape[0], 1), jnp.int32)
    for bit in range(30, -1, -1):
        cand = thr | (1 << bit)
        thr = jnp.where(count(bits >= cand) >= cap, cand, thr)
    above = bits > thr
    tie = bits == thr
    need = cap - count(above)
    tri = tri_ref[...]
    tie_rank = _dot(jnp.where(tie, 1.0, 0.0).astype(BF16), tri)
    sel = above | (tie & (tie_rank < need))
    pos = _dot(jnp.where(sel, 1.0, 0.0).astype(BF16), tri)
    slot_ref[...] = jnp.where(sel, pos.astype(jnp.int32), -1)


def _route(aff_t, cap):
    rows, n = aff_t.shape
    tri = jnp.triu(jnp.ones((n, n), BF16), k=1)
    return pl.pallas_call(
        functools.partial(_route_body, cap=cap),
        out_shape=jax.ShapeDtypeStruct((rows, n), jnp.int32),
        compiler_params=pltpu.CompilerParams(vmem_limit_bytes=VMEM_LIMIT),
        name="route",
    )(aff_t, tri)


def _experts_body(h_ref, slot_row_ref, aff_row_ref, slot_tok_ref, wg_ref, wu_ref, wd_ref, o_ref, *, cap):
    e = pl.program_id(1)
    group, n, _ = h_ref.shape
    xs, aff = [], []
    for s in range(group):
        pick = slot_row_ref[s, 0] == lax.broadcasted_iota(jnp.int32, (cap, n), 0)
        xs.append(_dot(jnp.where(pick, 1.0, 0.0).astype(BF16), h_ref[s]).astype(BF16))
        aff.append(jnp.sum(jnp.where(pick, aff_row_ref[s, 0], 0.0), axis=1, keepdims=True))
    xs = xs[0] if group == 1 else jnp.concatenate(xs, axis=0)
    aff = aff[0] if group == 1 else jnp.concatenate(aff, axis=0)
    hid = _dot(xs, wg_ref[0].astype(BF16))
    hid = hid * _sigmoid(hid) * _dot(xs, wu_ref[0].astype(BF16))
    ys = (_dot(hid.astype(BF16), wd_ref[0].astype(BF16)) * aff).astype(BF16)

    @pl.when(e == 0)
    def _():
        o_ref[...] = jnp.zeros_like(o_ref)

    is_e = lax.broadcasted_iota(jnp.int32, (n, LANES), 1) == e
    for s in range(group):
        slot_col = jnp.sum(jnp.where(is_e, slot_tok_ref[s], 0.0), axis=1, keepdims=True)
        put = jnp.where(slot_col == lax.broadcasted_iota(jnp.int32, (n, cap), 1).astype(F32), 1.0, 0.0)
        o_ref[s] += _dot(put.astype(BF16), ys[s * cap:(s + 1) * cap])


def _experts(h2, slot_t, aff_t, wg, wu, wd, cap):
    b, n, d = h2.shape
    ne, _, f = wg.shape
    group = min(b, max(1, MXU_ROWS // cap))
    while b % group:
        group -= 1
    slot_rows = slot_t.reshape(b, ne, 1, n)
    aff_rows = aff_t.reshape(b, ne, 1, n)
    slot_tok = jnp.transpose(slot_t.reshape(b, ne, n), (0, 2, 1)).astype(F32)
    slot_tok = jnp.pad(slot_tok, ((0, 0), (0, 0), (0, LANES - ne)))
    row = pl.BlockSpec((group, 1, 1, n), lambda bb, e: (bb, e, 0, 0))
    return pl.pallas_call(
        functools.partial(_experts_body, cap=cap),
        out_shape=jax.ShapeDtypeStruct((b, n, d), F32),
        grid=(b // group, ne),
        in_specs=[pl.BlockSpec((group, n, d), lambda bb, e: (bb, 0, 0)), row, row,
                  pl.BlockSpec((group, n, LANES), lambda bb, e: (bb, 0, 0)),
                  pl.BlockSpec((1, d, f), lambda bb, e: (e, 0, 0)),
                  pl.BlockSpec((1, d, f), lambda bb, e: (e, 0, 0)),
                  pl.BlockSpec((1, f, d), lambda bb, e: (e, 0, 0))],
        out_specs=pl.BlockSpec((group, n, d), lambda bb, e: (bb, 0, 0)),
        compiler_params=_params("arbitrary", "arbitrary"),
        name="experts",
    )(h2, slot_rows, aff_rows, slot_tok, wg, wu, wd)


def _moe(h2, aff, wg, wu, wd):
    b, n, _ = h2.shape
    cap = CAPACITY_FACTOR * n // N_EXPERTS
    aff_t = jnp.transpose(aff[:, :, :N_EXPERTS], (0, 2, 1)).reshape(b * N_EXPERTS, n)
    slot_t = _route(aff_t, cap)
    return _experts(h2, slot_t, aff_t, wg, wu, wd, cap)


def _ln2_body(x_ref, y_ref, g2_ref, lng_ref, lnb_ref, o_ref, *, alpha):
    o_ref[0] = _layer_norm(alpha * x_ref[0] + g2_ref[0] * y_ref[0], lng_ref[...], lnb_ref[...])


def _ln2(x1, y, g2, lng, lnb, alpha, tm):
    b, n, d = x1.shape
    tm = min(tm, n)
    tok = pl.BlockSpec((1, tm, d), lambda bb, i: (bb, i, 0))
    vec = pl.BlockSpec((1, d), lambda bb, i: (0, 0))
    return pl.pallas_call(
        functools.partial(_ln2_body, alpha=alpha),
        out_shape=jax.ShapeDtypeStruct((b, n, d), F32),
        grid=(b, n // tm),
        in_specs=[tok, tok, pl.BlockSpec((1, 1, d), lambda bb, i: (bb, 0, 0)), vec, vec],
        out_specs=tok,
        compiler_params=_params("arbitrary", "arbitrary"),
        name="ln2",
    )(x1, y, g2, lng, lnb)


def _rope_tables(n, enabled):
    lane = np.arange(LANES)
    if not enabled:
        one, zero = jnp.ones((n, LANES), F32), jnp.zeros((n, LANES), F32)
        return one, zero, one, zero
    t = jnp.arange(n)
    pos = jnp.stack([t // GRID_W, t % GRID_W], axis=1).astype(F32)

    def build(jj, d, active):
        freqs = ROPE_THETA ** (-jnp.arange(d, dtype=F32) / d)
        ang = pos[:, (jj // (2 * d)) % 2] * freqs[jj % d][None, :]
        sign = np.where((jj // d) % 2 == 0, -1.0, 1.0).astype(np.float32)
        cos = jnp.where(active[None, :], jnp.cos(ang), 1.0)
        sin = jnp.where(active[None, :], jnp.sin(ang) * sign[None, :], 0.0)
        return cos, sin

    cos_a, sin_a = build(lane % HEAD_DIM, HEAD_DIM // 4, np.ones(LANES, bool))
    in_rope = (lane >= B_NOPE) & (lane < B_NOPE + B_ROPE)
    cos_b, sin_b = build(np.where(in_rope, lane - B_NOPE, 0), B_ROPE // 4, in_rope)
    return cos_a, sin_a, cos_b, sin_b


def _pair_heads(w, axis, heads, groups):
    shape = w.shape
    split = shape[:axis] + (groups, heads // groups, HEAD_DIM) + shape[axis + 1:]
    return jnp.swapaxes(w.reshape(split), axis, axis + 1).reshape(shape)


def _layer_weights(w_in, b_q_norm, b_kv_norm, b_w_uq, b_w_ukv, c_q_norm, c_k_norm):
    d = w_in.shape[0]
    sizes = (A_HEADS * HEAD_DIM, A_KV_HEADS * HEAD_DIM, A_KV_HEADS * HEAD_DIM, B_Q_LORA, B_KV_LORA, B_ROPE,
             C_HEADS * HEAD_DIM, C_KV_HEADS * HEAD_DIM, C_KV_HEADS * HEAD_DIM)
    offs = np.cumsum((0,) + sizes)
    a_q, a_k, a_v, b_cq, b_ckv, b_kr, c_q, c_k, c_v = [w_in[:, offs[i]:offs[i + 1]] for i in range(9)]
    gate = w_in[:, offs[9]:]
    zeros = lambda w: jnp.zeros((d, w), w_in.dtype)
    w1 = jnp.concatenate([_pair_heads(a_q, 1, A_HEADS, A_KV_HEADS), a_k], axis=1)
    w2 = jnp.concatenate([_pair_heads(c_q, 1, C_HEADS, C_KV_HEADS), c_k], axis=1)
    w3 = jnp.concatenate([a_v, c_v], axis=1)
    w4 = jnp.concatenate([b_cq, b_ckv, zeros(B_NOPE), b_kr, zeros(B_HEAD_PAD - B_NOPE - B_ROPE)], axis=1)
    uq = b_w_uq.reshape(B_Q_LORA, B_HEADS, B_NOPE + B_ROPE)
    uq = jnp.pad(uq, ((0, 0), (0, 0), (0, B_HEAD_PAD - B_NOPE - B_ROPE))).reshape(B_Q_LORA, B_HEADS * B_HEAD_PAD)
    ukv = b_w_ukv.reshape(B_KV_LORA, B_HEADS, B_NOPE + B_V)
    uk = jnp.pad(ukv[:, :, :B_NOPE], ((0, 0), (0, 0), (0, B_HEAD_PAD - B_NOPE))).reshape(B_KV_LORA, B_HEADS * B_HEAD_PAD)
    uv = ukv[:, :, B_NOPE:].reshape(B_KV_LORA, B_HEADS * B_V)
    wuk = jnp.concatenate([uk, uv], axis=1)
    width = w2.shape[1]
    blk = np.arange(width) // HEAD_DIM
    bd = jnp.asarray((blk[:, None] == blk[None, :]).astype(np.float32) / HEAD_DIM, BF16)
    gc = jnp.concatenate([jnp.tile(c_q_norm, C_HEADS), jnp.tile(c_k_norm, C_KV_HEADS)])[None, :]
    cast = lambda a: a.astype(BF16)
    return (cast(w1), cast(w2), cast(w3), cast(w4), cast(uq), cast(wuk), bd,
            b_q_norm[None, :], b_kv_norm[None, :], gc), cast(gate)


def kernel(x, c, ctx, c_ctx, w_ada, b_ada, w_in, a_sink, b_q_norm, b_kv_norm, b_w_uq, b_w_ukv, c_q_norm, c_k_norm,
           w_branch, w_out, ln1_g, ln1_b, w_router, w_gate, w_up, w_down, ln2_g, ln2_b):
    bsz, n, d = x.shape
    depth = w_ada.shape[0]
    alpha = (2 * depth) ** 0.25
    aq, bo = A_HEADS * HEAD_DIM, B_HEADS * B_V
    tm = 512

    rows = -(-(bsz + 1) // 8) * 8
    cc = jnp.concatenate([c, c_ctx[None, :], jnp.zeros((rows - bsz - 1, d), c.dtype)], axis=0)
    mod = _ada(cc, w_ada, b_ada)
    tabs_lat = _rope_tables(n, True)
    tabs_ctx = _rope_tables(ctx.shape[1], False)

    b_kw = dict(hq=B_HEADS, hkv=B_HEADS, dq=B_HEAD_PAD, dv=B_V, tq=1024)
    c_kw = dict(hq=C_HEADS, hkv=C_KV_HEADS, dq=HEAD_DIM, dv=HEAD_DIM, tq=512)
    a_kw = dict(hq=A_HEADS, hkv=A_KV_HEADS, dq=HEAD_DIM, dv=HEAD_DIM, tq=256)

    cx = ctx
    for l in range(depth):
        last = l == depth - 1
        lat_mod = [m[:, None, :] for m in jnp.split(mod[l, :bsz], 6, axis=-1)]
        ctx_mod = [jnp.broadcast_to(m[None, None, :], (bsz, 1, d)) for m in jnp.split(mod[l, bsz], 6, axis=-1)]
        sh1, sc1, g1, sh2, sc2, g2 = lat_mod
        sh1c, sc1c, g1c, sh2c, sc2c, g2c = ctx_mod
        wts, w_gl = _layer_weights(w_in[l], b_q_norm[l], b_kv_norm[l], b_w_uq[l], b_w_ukv[l], c_q_norm[l], c_k_norm[l])
        wbr = w_branch[l].astype(BF16)
        merge_w = (w_gl, _pair_heads(wbr[:aq], 0, A_HEADS, A_KV_HEADS), wbr[aq:aq + bo],
                   _pair_heads(wbr[aq + bo:], 0, C_HEADS, C_KV_HEADS), w_out[l].astype(BF16),
                   jnp.pad(w_router[l], ((0, 0), (0, LANES - N_EXPERTS))).astype(BF16))
        ffn_w = (w_gate[l], w_up[l], w_down[l])
        ln1 = (ln1_g[l][None, :], ln1_b[l][None, :])
        ln2 = (ln2_g[l][None, :], ln2_b[l][None, :])

        qa, ka, va, qb, kb, vb, qc, kc, vc = _inproj(x, sc1, sh1, wts, tabs_lat, tm)
        cqa, cka, cva, cqb, ckb, cvb, cqc, ckc, cvc = _inproj(cx, sc1c, sh1c, wts, tabs_ctx, tm)

        o_a = _attn_a(qa, ka, va, cka, cva, a_sink[l], min(n, 256))
        o_b = _attn(qb, [(ckb, cvb), (kb, vb)], None, name="attn_b", **b_kw)
        o_c = _attn(qc, [(ckc, cvc), (kc, vc)], None, name="attn_c", **c_kw)
        x1, h2, aff = _merge(o_a, o_b, o_c, x, sc1, sh1, g1, sc2, sh2, *ln1, *merge_w, alpha, tm)
        if not last:
            oc_a = _attn(cqa, [(cka, cva)], a_sink[l], name="attn_a_ctx", **a_kw)
            oc_b = _attn(cqb, [(ckb, cvb)], None, name="attn_b_ctx", **b_kw)
            oc_c = _attn(cqc, [(ckc, cvc)], None, name="attn_c_ctx", **c_kw)
            cx1, hc2, affc = _merge(oc_a, oc_b, oc_c, cx, sc1c, sh1c, g1c, sc2c, sh2c, *ln1, *merge_w, alpha, tm)

        x = _ln2(x1, _moe(h2, aff, *ffn_w), g2, *ln2, alpha, tm)
        if not last:
            cx = _ln2(cx1, _moe(hc2, affc, *ffn_w), g2c, *ln2, alpha, tm)
    return x
```

```python
import functools

import jax
import jax.numpy as jnp
import numpy as np
from jax import lax
from jax.experimental import pallas as pl
from jax.experimental.pallas import tpu as pltpu

F32 = jnp.float32
BF16 = jnp.bfloat16

GRID_W = 64
HEAD_DIM = 64
A_HEADS, A_KV_HEADS = 6, 2
WINDOW = 128
B_HEADS, B_NOPE, B_ROPE, B_V = 4, 64, 32, 64
B_Q_LORA, B_KV_LORA = 256, 128
C_HEADS, C_KV_HEADS = 6, 2
N_EXPERTS = 16
CAPACITY_FACTOR = 2
ROPE_THETA = 10000.0
LN_EPS = 1e-5
RMS_EPS = 1e-6
LOG2E = float(np.log2(np.e))
A_SCALE = HEAD_DIM ** -0.5 * LOG2E
B_SCALE = (B_NOPE + B_ROPE) ** -0.5 * LOG2E
C_SCALE = HEAD_DIM ** -0.5 * LOG2E

LANES = 128
BF16_SUBLANES = 16
MXU_ROWS = 256
B_HEAD_PAD = 128
NEG_BIG = -1e30
VMEM_LIMIT = 56 * 1024 * 1024


def _params(*sem):
    return pltpu.CompilerParams(dimension_semantics=sem, vmem_limit_bytes=VMEM_LIMIT)


def _dot(a, b):
    return jnp.dot(a, b, preferred_element_type=F32)


def _dot_nt(a, b):
    return lax.dot_general(a, b, (((1,), (1,)), ((), ())), preferred_element_type=F32)


def _ada_body(c_ref, w_ref, b_ref, o_ref):
    c = c_ref[...]
    s = (c * (1.0 / (1.0 + jnp.exp(-c)))).astype(BF16)
    o_ref[0] = _dot(s, w_ref[0].astype(BF16)) + b_ref[0]


def _ada(cc, w_ada, b_ada):
    depth, d, six_d = w_ada.shape
    rows = cc.shape[0]
    tn = six_d // 4
    return pl.pallas_call(
        _ada_body,
        out_shape=jax.ShapeDtypeStruct((depth, rows, six_d), F32),
        grid=(depth, six_d // tn),
        in_specs=[
            pl.BlockSpec((rows, d), lambda l, j: (0, 0)),
            pl.BlockSpec((1, d, tn), lambda l, j: (l, 0, j)),
            pl.BlockSpec((1, 1, tn), lambda l, j: (l, 0, j)),
        ],
        out_specs=pl.BlockSpec((1, rows, tn), lambda l, j: (l, 0, j)),
        compiler_params=_params("arbitrary", "arbitrary"),
        name="ada",
    )(cc, w_ada, b_ada.reshape(depth, 1, six_d))


def _rope(x, cos, sin, d):
    lane = lax.broadcasted_iota(jnp.int32, (x.shape[0], LANES), 1)
    first = (lane // d) % 2 == 0
    outs = []
    for j in range(x.shape[1] // LANES):
        xs = x[:, j * LANES:(j + 1) * LANES]
        partner = jnp.where(first, pltpu.roll(xs, LANES - d, 1), pltpu.roll(xs, d, 1))
        outs.append(xs * cos + partner * sin)
    return outs[0] if len(outs) == 1 else jnp.concatenate(outs, axis=1)


def _rms(x, gain):
    return x * lax.rsqrt(jnp.mean(x * x, axis=-1, keepdims=True) + RMS_EPS) * gain


def _inproj_body(x_ref, sc_ref, sh_ref, w_ref, wuq_ref, wuk_ref, bd_ref,
                 gq_ref, gkv_ref, gc_ref, cos_a_ref, sin_a_ref, cos_b_ref, sin_b_ref,
                 qa_ref, ka_ref, va_ref, qb_ref, kb_ref, vb_ref, qc_ref, kc_ref, vc_ref):
    h = (x_ref[0] * (1.0 + sc_ref[0]) + sh_ref[0]).astype(BF16)
    cos_a, sin_a = cos_a_ref[...], sin_a_ref[...]
    cos_b, sin_b = cos_b_ref[...], sin_b_ref[...]
    aq = A_HEADS * HEAD_DIM
    cq = C_HEADS * HEAD_DIM
    proj = _dot(h, w_ref[...])
    o_c, o_v, o_b = aq + LANES, aq + cq + 2 * LANES, aq + cq + 4 * LANES

    r = _rope(proj[:, :o_c], cos_a, sin_a, HEAD_DIM // 4)
    qa_ref[0] = (r[:, :aq] * A_SCALE).astype(BF16)
    ka_ref[0] = r[:, aq:].astype(BF16)

    r = proj[:, o_c:o_v]
    sq = r * r
    sq_hi = sq.astype(BF16)
    sq_lo = (sq - sq_hi.astype(F32)).astype(BF16)
    ms = _dot(sq_hi, bd_ref[...]) + _dot(sq_lo, bd_ref[...])
    r = _rope(r * lax.rsqrt(ms + RMS_EPS) * gc_ref[...], cos_a, sin_a, HEAD_DIM // 4)
    qc_ref[0] = (r[:, :cq] * C_SCALE).astype(BF16)
    kc_ref[0] = r[:, cq:].astype(BF16)

    r = proj[:, o_v:o_b].T
    va_ref[0] = r[:LANES].astype(BF16)
    vc_ref[0] = r[LANES:].astype(BF16)

    r = proj[:, o_b:]
    c_q = _rms(r[:, :B_Q_LORA], gq_ref[...]).astype(BF16)
    c_kv = _rms(r[:, B_Q_LORA:B_Q_LORA + B_KV_LORA], gkv_ref[...]).astype(BF16)
    k_rope = _rope(r[:, B_Q_LORA + B_KV_LORA:], cos_b, sin_b, B_ROPE // 4)
    q = _rope(_dot(c_q, wuq_ref[...]), cos_b, sin_b, B_ROPE // 4)
    qb_ref[0] = (q * B_SCALE).astype(BF16)
    kv = _dot(c_kv, wuk_ref[...])
    kw = B_HEADS * B_HEAD_PAD
    kb_ref[0] = (kv[:, :kw] + jnp.concatenate([k_rope] * B_HEADS, axis=1)).astype(BF16)
    vb_ref[0] = kv[:, kw:].T.astype(BF16)


def _inproj(x, sc, sh, wts, tabs, tm):
    b, n, d = x.shape
    tm = min(tm, n)
    cos_a, sin_a, cos_b, sin_b = tabs
    full = lambda a: pl.BlockSpec(a.shape, lambda i, bb: (0,) * a.ndim, pipeline_mode=pl.Buffered(1))
    tok = lambda w: pl.BlockSpec((1, tm, w), lambda i, bb: (bb, i, 0))
    mod = pl.BlockSpec((1, 1, d), lambda i, bb: (bb, 0, 0))
    tab = pl.BlockSpec((tm, LANES), lambda i, bb: (i, 0))
    widths = (A_HEADS * HEAD_DIM, LANES, -LANES, B_HEADS * B_HEAD_PAD, B_HEADS * B_HEAD_PAD, -B_HEADS * B_V,
              C_HEADS * HEAD_DIM, LANES, -LANES)
    tok_t = lambda w: pl.BlockSpec((1, w, tm), lambda i, bb: (bb, 0, i))
    return pl.pallas_call(
        _inproj_body,
        out_shape=[jax.ShapeDtypeStruct((b, n, w) if w > 0 else (b, -w, n), BF16) for w in widths],
        grid=(n // tm, b),
        in_specs=[tok(d), mod, mod] + [full(a) for a in wts] + [tab] * 4,
        out_specs=[tok(w) if w > 0 else tok_t(-w) for w in widths],
        compiler_params=_params("arbitrary", "arbitrary"),
        name="inproj",
    )(x, sc, sh, *wts, cos_a, sin_a, cos_b, sin_b)


def _attend_t(q, segs, sink):
    scores = []
    for k, _, valid in segs:
        st = _dot_nt(k, q)
        scores.append(st if valid is None else jnp.where(valid, st, NEG_BIG))
    m = scores[0].max(axis=0, keepdims=True)
    for st in scores[1:]:
        m = jnp.maximum(m, st.max(axis=0, keepdims=True))
    if sink is not None:
        m = jnp.maximum(m, sink)
    out_t = 0.0
    for st, (_, vt, _) in zip(scores, segs):
        vt1 = jnp.concatenate([vt, jnp.ones((BF16_SUBLANES, vt.shape[1]), BF16)], axis=0)
        out_t = out_t + _dot(vt1, jnp.exp2(st - m).astype(BF16))
    den = out_t[LANES:LANES + 1]
    if sink is not None:
        den = den + jnp.exp2(sink - m)
    return (out_t[:LANES] / den).T


def _attn_plan(hq, hkv, dq):
    if dq == LANES:
        return [([h], None, h * LANES, (h // 2) * LANES, [h], h % 2, [h // 2]) for h in range(hq)]
    rep = hq // hkv
    assert hkv == 2 and dq == LANES // 2
    return [(list(range(rep)), g, 0, 0, [g * rep + r for r in range(rep)], g, list(range(rep))) for g in range(hkv)]


def _attn_units(q, segs_of, sink_ref, plan):
    tq = q.shape[0]
    low_half = lax.broadcasted_iota(jnp.int32, (tq, LANES), 1) < LANES // 2
    pieces = {}
    for tiles, keep, k0, v0, heads, out_half, out_tiles in plan:
        qs = []
        for t in tiles:
            qt = q[:, t * LANES:(t + 1) * LANES]
            if keep is not None:
                qt = jnp.where(low_half == (keep == 0), qt, jnp.zeros_like(qt))
            qs.append(qt)
        qg = qs[0] if len(qs) == 1 else jnp.concatenate(qs, axis=0)
        sink = None
        if sink_ref is not None:
            sink = jnp.concatenate([jnp.full((1, tq), sink_ref[hd] * LOG2E, F32) for hd in heads], axis=1)
        og = _attend_t(qg, segs_of(k0, v0), sink)
        for j, t in enumerate(out_tiles):
            pieces[(t, out_half)] = og[j * tq:(j + 1) * tq]
    n_out = 1 + max(t for t, _ in pieces)
    outs = [jnp.where(low_half, pieces[(t, 0)], pieces[(t, 1)]) for t in range(n_out)]
    return (outs[0] if n_out == 1 else jnp.concatenate(outs, axis=1)).astype(BF16)


def _attn_body(*refs, n_seg, plan, use_sink):
    q_ref = refs[0]
    kv_refs = refs[1:1 + 2 * n_seg]
    sink_ref = refs[1 + 2 * n_seg] if use_sink else None
    o_ref = refs[-1]

    def segs_of(k0, v0):
        return [(kv_refs[2 * s][0, :, k0:k0 + LANES], kv_refs[2 * s + 1][0, v0:v0 + LANES, :], None)
                for s in range(n_seg)]

    o_ref[0] = _attn_units(q_ref[0], segs_of, sink_ref, plan)


def _attn(q, segs, sink, *, hq, hkv, dq, dv, tq, name):
    b, n, _ = q.shape
    tq = min(tq, n)
    in_specs = [pl.BlockSpec((1, tq, hq * dq), lambda bb, i: (bb, i, 0))]
    args = [q]
    for k, vt in segs:
        in_specs.append(pl.BlockSpec((1,) + k.shape[1:], lambda bb, i: (bb, 0, 0)))
        in_specs.append(pl.BlockSpec((1,) + vt.shape[1:], lambda bb, i: (bb, 0, 0)))
        args += [k, vt]
    if sink is not None:
        in_specs.append(pl.BlockSpec(memory_space=pltpu.SMEM))
        args.append(sink)
    body = functools.partial(_attn_body, n_seg=len(segs), plan=_attn_plan(hq, hkv, dq), use_sink=sink is not None)
    return pl.pallas_call(
        body,
        out_shape=jax.ShapeDtypeStruct((b, n, hq * dv), BF16),
        grid=(b, n // tq),
        in_specs=in_specs,
        out_specs=pl.BlockSpec((1, tq, hq * dv), lambda bb, i: (bb, i, 0)),
        compiler_params=_params("arbitrary", "arbitrary"),
        name=name,
    )(*args)


def _attn_a_body(*refs, seq, tq, nkb):
    q_ref = refs[0]
    k_refs, v_refs = refs[1:1 + nkb], refs[1 + nkb:1 + 2 * nkb]
    ck_ref, cvt_ref, sink_ref, o_ref = refs[1 + 2 * nkb:]
    start = pl.program_id(1) * tq
    span = nkb * WINDOW
    k_win = jnp.concatenate([r[0] for r in k_refs], axis=0)
    vt_win = jnp.concatenate([r[0] for r in v_refs], axis=1)
    kpos = start - WINDOW + lax.broadcasted_iota(jnp.int32, (span, tq), 0)
    qpos = start + lax.broadcasted_iota(jnp.int32, (span, tq), 1)
    valid = (kpos >= 0) & (kpos < seq) & (jnp.abs(qpos - kpos) <= WINDOW)
    valid = jnp.concatenate([valid] * (A_HEADS // A_KV_HEADS), axis=1)

    def segs_of(k0, v0):
        return [(k_win, vt_win, valid), (ck_ref[0], cvt_ref[0], None)]

    o_ref[0] = _attn_units(q_ref[0], segs_of, sink_ref, _attn_plan(A_HEADS, A_KV_HEADS, HEAD_DIM))


def _attn_a(q, k, vt, ck, cvt, sink, tq):
    b, n, w = q.shape
    assert WINDOW == LANES and tq % WINDOW == 0 and n % tq == 0
    per = tq // WINDOW
    nkb = per + 2
    last = n // WINDOW - 1
    blk = lambda j: (lambda i: jnp.clip(i * per + j - 1, 0, last))
    k_blk = lambda f: pl.BlockSpec((1, WINDOW, LANES), lambda bb, i: (bb, f(i), 0))
    v_blk = lambda f: pl.BlockSpec((1, LANES, WINDOW), lambda bb, i: (bb, 0, f(i)))
    whole = lambda a: pl.BlockSpec((1,) + a.shape[1:], lambda bb, i: (bb, 0, 0))
    return pl.pallas_call(
        functools.partial(_attn_a_body, seq=n, tq=tq, nkb=nkb),
        out_shape=jax.ShapeDtypeStruct((b, n, w), BF16),
        grid=(b, n // tq),
        in_specs=[pl.BlockSpec((1, tq, w), lambda bb, i: (bb, i, 0))]
                 + [k_blk(blk(j)) for j in range(nkb)] + [v_blk(blk(j)) for j in range(nkb)]
                 + [whole(ck), whole(cvt), pl.BlockSpec(memory_space=pltpu.SMEM)],
        out_specs=pl.BlockSpec((1, tq, w), lambda bb, i: (bb, i, 0)),
        compiler_params=_params("arbitrary", "arbitrary"),
        name="attn_a_window",
    )(q, *([k] * nkb), *([vt] * nkb), ck, cvt, sink)


def _sigmoid(z):
    return 1.0 / (1.0 + jnp.exp(-z))


def _layer_norm(z, g, b):
    mu = jnp.mean(z, axis=-1, keepdims=True)
    zc = z - mu
    var = jnp.mean(zc * zc, axis=-1, keepdims=True)
    return zc * lax.rsqrt(var + LN_EPS) * g + b


def _merge_body(oa_ref, ob_ref, oc_ref, x_ref, sc1_ref, sh1_ref, g1_ref, sc2_ref, sh2_ref, lng_ref, lnb_ref,
                wg_ref, wa_ref, wb_ref, wc_ref, wo_ref, wr_ref, x1_ref, h2_ref, aff_ref, *, alpha):
    d = x_ref.shape[-1]
    x = x_ref[0]
    h1 = (x * (1.0 + sc1_ref[0]) + sh1_ref[0]).astype(BF16)
    merged = 0.0
    for j, (o_ref, w_ref) in enumerate(((oa_ref, wa_ref), (ob_ref, wb_ref), (oc_ref, wc_ref))):
        gate = _sigmoid(_dot(h1, wg_ref[:, j * d:(j + 1) * d]))
        merged = merged + gate * _dot(o_ref[0], w_ref[...])
    y = _dot(merged.astype(BF16), wo_ref[...])
    x1 = _layer_norm(alpha * x + g1_ref[0] * y, lng_ref[...], lnb_ref[...])
    x1_ref[0] = x1
    h2 = (x1 * (1.0 + sc2_ref[0]) + sh2_ref[0]).astype(BF16)
    h2_ref[0] = h2
    logits = _dot(h2, wr_ref[...])
    lane = lax.broadcasted_iota(jnp.int32, logits.shape, 1)
    logits = jnp.where(lane < N_EXPERTS, logits, NEG_BIG)
    e = jnp.exp(logits - logits.max(axis=-1, keepdims=True))
    aff_ref[0] = e / e.sum(axis=-1, keepdims=True)


def _merge(oa, ob, oc, x, sc1, sh1, g1, sc2, sh2, lng, lnb, wg, wa, wb, wc, wo, wr, alpha, tm):
    b, n, d = x.shape
    tm = min(tm, n)
    tok = lambda w: pl.BlockSpec((1, tm, w), lambda bb, i: (bb, i, 0))
    mod = pl.BlockSpec((1, 1, d), lambda bb, i: (bb, 0, 0))
    full = lambda a: pl.BlockSpec(a.shape, lambda bb, i: (0,) * a.ndim, pipeline_mode=pl.Buffered(1))
    return pl.pallas_call(
        functools.partial(_merge_body, alpha=alpha),
        out_shape=[jax.ShapeDtypeStruct((b, n, d), F32), jax.ShapeDtypeStruct((b, n, d), BF16),
                   jax.ShapeDtypeStruct((b, n, LANES), F32)],
        grid=(b, n // tm),
        in_specs=[tok(oa.shape[2]), tok(ob.shape[2]), tok(oc.shape[2]), tok(d), mod, mod, mod, mod, mod,
                  full(lng), full(lnb), full(wg), full(wa), full(wb), full(wc), full(wo), full(wr)],
        out_specs=[tok(d), tok(d), tok(LANES)],
        compiler_params=_params("arbitrary", "arbitrary"),
        name="merge",
    )(oa, ob, oc, x, sc1, sh1, g1, sc2, sh2, lng, lnb, wg, wa, wb, wc, wo, wr)


def _route_body(aff_ref, tri_ref, slot_ref, *, cap):
    bits = lax.bitcast_convert_type(aff_ref[...], jnp.int32)
    count = lambda m: jnp.sum(jnp.where(m, 1.0, 0.0), axis=1, keepdims=True)
    thr = jnp.zeros((bits.shape[0], 1), jnp.int32)
    for bit in range(30, -1, -1):
        cand = thr | (1 << bit)
        thr = jnp.where(count(bits >= cand) >= cap, cand, thr)
    above = bits > thr
    tie = bits == thr
    need = cap - count(above)
    tri = tri_ref[...]
    tie_rank = _dot(jnp.where(tie, 1.0, 0.0).astype(BF16), tri)
    sel = above | (tie & (tie_rank < need))
    pos = _dot(jnp.where(sel, 1.0, 0.0).astype(BF16), tri)
    slot_ref[...] = jnp.where(sel, pos.astype(jnp.int32), -1)


def _route(aff_t, cap):
    rows, n = aff_t.shape
    tri = jnp.triu(jnp.ones((n, n), BF16), k=1)
    return pl.pallas_call(
        functools.partial(_route_body, cap=cap),
        out_shape=jax.ShapeDtypeStruct((rows, n), jnp.int32),
        compiler_params=pltpu.CompilerParams(vmem_limit_bytes=VMEM_LIMIT),
        name="route",
    )(aff_t, tri)


def _experts_body(h_ref, slot_row_ref, aff_row_ref, slot_tok_ref, wg_ref, wu_ref, wd_ref, o_ref, *, cap):
    e = pl.program_id(1)
    group, n, _ = h_ref.shape
    xs, aff = [], []
    for s in range(group):
        pick = slot_row_ref[s, 0] == lax.broadcasted_iota(jnp.int32, (cap, n), 0)
        xs.append(_dot(jnp.where(pick, 1.0, 0.0).astype(BF16), h_ref[s]).astype(BF16))
        aff.append(jnp.sum(jnp.where(pick, aff_row_ref[s, 0], 0.0), axis=1, keepdims=True))
    xs = xs[0] if group == 1 else jnp.concatenate(xs, axis=0)
    aff = aff[0] if group == 1 else jnp.concatenate(aff, axis=0)
    hid = _dot(xs, wg_ref[0, 0].astype(BF16))
    hid = hid * _sigmoid(hid) * _dot(xs, wu_ref[0, 0].astype(BF16))
    ys = (_dot(hid.astype(BF16), wd_ref[0, 0].astype(BF16)) * aff).astype(BF16)

    @pl.when(e == 0)
    def _():
        o_ref[...] = jnp.zeros_like(o_ref)

    is_e = lax.broadcasted_iota(jnp.int32, (n, LANES), 1) == e
    for s in range(group):
        slot_col = jnp.sum(jnp.where(is_e, slot_tok_ref[s], 0.0), axis=1, keepdims=True)
        put = jnp.where(slot_col == lax.broadcasted_iota(jnp.int32, (n, cap), 1).astype(F32), 1.0, 0.0)
        o_ref[s] += _dot(put.astype(BF16), ys[s * cap:(s + 1) * cap])


def _experts(h2, slot_t, aff_t, wg, wu, wd, layer, cap):
    b, n, d = h2.shape
    _, ne, _, f = wg.shape
    group = min(b, max(1, MXU_ROWS // cap))
    while b % group:
        group -= 1
    slot_rows = slot_t.reshape(b, ne, 1, n)
    aff_rows = aff_t.reshape(b, ne, 1, n)
    slot_tok = jnp.transpose(slot_t.reshape(b, ne, n), (0, 2, 1)).astype(F32)
    slot_tok = jnp.pad(slot_tok, ((0, 0), (0, 0), (0, LANES - ne)))
    row = pl.BlockSpec((group, 1, 1, n), lambda bb, e: (bb, e, 0, 0))
    return pl.pallas_call(
        functools.partial(_experts_body, cap=cap),
        out_shape=jax.ShapeDtypeStruct((b, n, d), F32),
        grid=(b // group, ne),
        in_specs=[pl.BlockSpec((group, n, d), lambda bb, e: (bb, 0, 0)), row, row,
                  pl.BlockSpec((group, n, LANES), lambda bb, e: (bb, 0, 0)),
                  pl.BlockSpec((1, 1, d, f), lambda bb, e: (layer, e, 0, 0)),
                  pl.BlockSpec((1, 1, d, f), lambda bb, e: (layer, e, 0, 0)),
                  pl.BlockSpec((1, 1, f, d), lambda bb, e: (layer, e, 0, 0))],
        out_specs=pl.BlockSpec((group, n, d), lambda bb, e: (bb, 0, 0)),
        compiler_params=_params("arbitrary", "arbitrary"),
        name="experts",
    )(h2, slot_rows, aff_rows, slot_tok, wg, wu, wd)


def _moe(h2, aff, wg, wu, wd, layer):
    b, n, _ = h2.shape
    cap = CAPACITY_FACTOR * n // N_EXPERTS
    aff_t = jnp.transpose(aff[:, :, :N_EXPERTS], (0, 2, 1)).reshape(b * N_EXPERTS, n)
    slot_t = _route(aff_t, cap)
    return _experts(h2, slot_t, aff_t, wg, wu, wd, layer, cap)


def _ln2_body(x_ref, y_ref, g2_ref, lng_ref, lnb_ref, o_ref, *, alpha):
    o_ref[0] = _layer_norm(alpha * x_ref[0] + g2_ref[0] * y_ref[0], lng_ref[...], lnb_ref[...])


def _ln2(x1, y, g2, lng, lnb, alpha, tm):
    b, n, d = x1.shape
    tm = min(tm, n)
    tok = pl.BlockSpec((1, tm, d), lambda bb, i: (bb, i, 0))
    vec = pl.BlockSpec((1, d), lambda bb, i: (0, 0))
    return pl.pallas_call(
        functools.partial(_ln2_body, alpha=alpha),
        out_shape=jax.ShapeDtypeStruct((b, n, d), F32),
        grid=(b, n // tm),
        in_specs=[tok, tok, pl.BlockSpec((1, 1, d), lambda bb, i: (bb, 0, 0)), vec, vec],
        out_specs=tok,
        compiler_params=_params("arbitrary", "arbitrary"),
        name="ln2",
    )(x1, y, g2, lng, lnb)


def _rope_tables(n, enabled):
    lane = np.arange(LANES)
    if not enabled:
        one, zero = jnp.ones((n, LANES), F32), jnp.zeros((n, LANES), F32)
        return one, zero, one, zero
    t = jnp.arange(n)
    pos = jnp.stack([t // GRID_W, t % GRID_W], axis=1).astype(F32)

    def build(jj, d, active):
        freqs = ROPE_THETA ** (-jnp.arange(d, dtype=F32) / d)
        ang = pos[:, (jj // (2 * d)) % 2] * freqs[jj % d][None, :]
        sign = np.where((jj // d) % 2 == 0, -1.0, 1.0).astype(np.float32)
        cos = jnp.where(active[None, :], jnp.cos(ang), 1.0)
        sin = jnp.where(active[None, :], jnp.sin(ang) * sign[None, :], 0.0)
        return cos, sin

    cos_a, sin_a = build(lane % HEAD_DIM, HEAD_DIM // 4, np.ones(LANES, bool))
    in_rope = (lane >= B_NOPE) & (lane < B_NOPE + B_ROPE)
    cos_b, sin_b = build(np.where(in_rope, lane - B_NOPE, 0), B_ROPE // 4, in_rope)
    return cos_a, sin_a, cos_b, sin_b


def _pair_heads(w, axis, heads, groups):
    shape = w.shape
    split = shape[:axis] + (groups, heads // groups, HEAD_DIM) + shape[axis + 1:]
    return jnp.swapaxes(w.reshape(split), axis, axis + 1).reshape(shape)


def _layer_weights(w_in, b_q_norm, b_kv_norm, b_w_uq, b_w_ukv, c_q_norm, c_k_norm):
    d = w_in.shape[0]
    sizes = (A_HEADS * HEAD_DIM, A_KV_HEADS * HEAD_DIM, A_KV_HEADS * HEAD_DIM, B_Q_LORA, B_KV_LORA, B_ROPE,
             C_HEADS * HEAD_DIM, C_KV_HEADS * HEAD_DIM, C_KV_HEADS * HEAD_DIM)
    offs = np.cumsum((0,) + sizes)
    a_q, a_k, a_v, b_cq, b_ckv, b_kr, c_q, c_k, c_v = [w_in[:, offs[i]:offs[i + 1]] for i in range(9)]
    gate = w_in[:, offs[9]:]
    zeros = lambda w: jnp.zeros((d, w), w_in.dtype)
    w1 = jnp.concatenate([_pair_heads(a_q, 1, A_HEADS, A_KV_HEADS), a_k], axis=1)
    w2 = jnp.concatenate([_pair_heads(c_q, 1, C_HEADS, C_KV_HEADS), c_k], axis=1)
    w3 = jnp.concatenate([a_v, c_v], axis=1)
    w4 = jnp.concatenate([b_cq, b_ckv, zeros(B_NOPE), b_kr, zeros(B_HEAD_PAD - B_NOPE - B_ROPE)], axis=1)
    uq = b_w_uq.reshape(B_Q_LORA, B_HEADS, B_NOPE + B_ROPE)
    uq = jnp.pad(uq, ((0, 0), (0, 0), (0, B_HEAD_PAD - B_NOPE - B_ROPE))).reshape(B_Q_LORA, B_HEADS * B_HEAD_PAD)
    ukv = b_w_ukv.reshape(B_KV_LORA, B_HEADS, B_NOPE + B_V)
    uk = jnp.pad(ukv[:, :, :B_NOPE], ((0, 0), (0, 0), (0, B_HEAD_PAD - B_NOPE))).reshape(B_KV_LORA, B_HEADS * B_HEAD_PAD)
    uv = ukv[:, :, B_NOPE:].reshape(B_KV_LORA, B_HEADS * B_V)
    wuk = jnp.concatenate([uk, uv], axis=1)
    width = w2.shape[1]
    blk = np.arange(width) // HEAD_DIM
    bd = jnp.asarray((blk[:, None] == blk[None, :]).astype(np.float32) / HEAD_DIM, BF16)
    gc = jnp.concatenate([jnp.tile(c_q_norm, C_HEADS), jnp.tile(c_k_norm, C_KV_HEADS)])[None, :]
    cast = lambda a: a.astype(BF16)
    return (cast(jnp.concatenate([w1, w2, w3, w4], axis=1)), cast(uq), cast(wuk), bd,
            b_q_norm[None, :], b_kv_norm[None, :], gc), cast(gate)


def kernel(x, c, ctx, c_ctx, w_ada, b_ada, w_in, a_sink, b_q_norm, b_kv_norm, b_w_uq, b_w_ukv, c_q_norm, c_k_norm,
           w_branch, w_out, ln1_g, ln1_b, w_router, w_gate, w_up, w_down, ln2_g, ln2_b):
    bsz, n, d = x.shape
    depth = w_ada.shape[0]
    alpha = (2 * depth) ** 0.25
    aq, bo = A_HEADS * HEAD_DIM, B_HEADS * B_V
    tm = 512

    rows = -(-(bsz + 1) // 8) * 8
    cc = jnp.concatenate([c, c_ctx[None, :], jnp.zeros((rows - bsz - 1, d), c.dtype)], axis=0)
    mod = _ada(cc, w_ada, b_ada)
    tabs_lat = _rope_tables(n, True)
    tabs_ctx = _rope_tables(ctx.shape[1], False)

    b_kw = dict(hq=B_HEADS, hkv=B_HEADS, dq=B_HEAD_PAD, dv=B_V, tq=1024)
    c_kw = dict(hq=C_HEADS, hkv=C_KV_HEADS, dq=HEAD_DIM, dv=HEAD_DIM, tq=512)
    a_kw = dict(hq=A_HEADS, hkv=A_KV_HEADS, dq=HEAD_DIM, dv=HEAD_DIM, tq=256)

    cx = ctx
    for l in range(depth):
        last = l == depth - 1
        lat_mod = [m[:, None, :] for m in jnp.split(mod[l, :bsz], 6, axis=-1)]
        ctx_mod = [jnp.broadcast_to(m[None, None, :], (bsz, 1, d)) for m in jnp.split(mod[l, bsz], 6, axis=-1)]
        sh1, sc1, g1, sh2, sc2, g2 = lat_mod
        sh1c, sc1c, g1c, sh2c, sc2c, g2c = ctx_mod
        wts, w_gl = _layer_weights(w_in[l], b_q_norm[l], b_kv_norm[l], b_w_uq[l], b_w_ukv[l], c_q_norm[l], c_k_norm[l])
        wbr = w_branch[l].astype(BF16)
        merge_w = (w_gl, _pair_heads(wbr[:aq], 0, A_HEADS, A_KV_HEADS), wbr[aq:aq + bo],
                   _pair_heads(wbr[aq + bo:], 0, C_HEADS, C_KV_HEADS), w_out[l].astype(BF16),
                   jnp.pad(w_router[l], ((0, 0), (0, LANES - N_EXPERTS))).astype(BF16))
        ffn_w = (w_gate, w_up, w_down, l)
        ln1 = (ln1_g[l][None, :], ln1_b[l][None, :])
        ln2 = (ln2_g[l][None, :], ln2_b[l][None, :])

        qa, ka, va, qb, kb, vb, qc, kc, vc = _inproj(x, sc1, sh1, wts, tabs_lat, 2 * tm)
        cqa, cka, cva, cqb, ckb, cvb, cqc, ckc, cvc = _inproj(cx, sc1c, sh1c, wts, tabs_ctx, tm)

        o_a = _attn_a(qa, ka, va, cka, cva, a_sink[l], min(n, 256))
        o_b = _attn(qb, [(ckb, cvb), (kb, vb)], None, name="attn_b", **b_kw)
        o_c = _attn(qc, [(ckc, cvc), (kc, vc)], None, name="attn_c", **c_kw)
        x1, h2, aff = _merge(o_a, o_b, o_c, x, sc1, sh1, g1, sc2, sh2, *ln1, *merge_w, alpha, tm)
        if not last:
            oc_a = _attn(cqa, [(cka, cva)], a_sink[l], name="attn_a_ctx", **a_kw)
            oc_b = _attn(cqb, [(ckb, cvb)], None, name="attn_b_ctx", **b_kw)
            oc_c = _attn(cqc, [(ckc, cvc)], None, name="attn_c_ctx", **c_kw)
            cx1, hc2, affc = _merge(oc_a, oc_b, oc_c, cx, sc1c, sh1c, g1c, sc2c, sh2c, *ln1, *merge_w, alpha, tm)

        x = _ln2(x1, _moe(h2, aff, *ffn_w), g2, *ln2, alpha, tm)
        if not last:
            cx = _ln2(cx1, _moe(hc2, affc, *ffn_w), g2c, *ln2, alpha, tm)
    return x
```

```python
import functools

import jax
import jax.numpy as jnp
import numpy as np
from jax import lax
from jax.experimental import pallas as pl
from jax.experimental.pallas import tpu as pltpu

F32 = jnp.float32
BF16 = jnp.bfloat16

GRID_W = 64
HEAD_DIM = 64
A_HEADS, A_KV_HEADS = 6, 2
WINDOW = 128
B_HEADS, B_NOPE, B_ROPE, B_V = 4, 64, 32, 64
B_Q_LORA, B_KV_LORA = 256, 128
C_HEADS, C_KV_HEADS = 6, 2
N_EXPERTS = 16
CAPACITY_FACTOR = 2
ROPE_THETA = 10000.0
LN_EPS = 1e-5
RMS_EPS = 1e-6
LOG2E = float(np.log2(np.e))
A_SCALE = HEAD_DIM ** -0.5 * LOG2E
B_SCALE = (B_NOPE + B_ROPE) ** -0.5 * LOG2E
C_SCALE = HEAD_DIM ** -0.5 * LOG2E

LANES = 128
BF16_SUBLANES = 16
MXU_ROWS = 256
B_HEAD_PAD = 128
NEG_BIG = -1e30
VMEM_LIMIT = 56 * 1024 * 1024


def _params(*sem):
    return pltpu.CompilerParams(dimension_semantics=sem, vmem_limit_bytes=VMEM_LIMIT)


def _dot(a, b):
    return jnp.dot(a, b, preferred_element_type=F32)


def _dot_nt(a, b):
    return lax.dot_general(a, b, (((1,), (1,)), ((), ())), preferred_element_type=F32)


def _ada_body(c_ref, w_ref, b_ref, o_ref):
    c = c_ref[...]
    s = (c * (1.0 / (1.0 + jnp.exp(-c)))).astype(BF16)
    o_ref[0] = _dot(s, w_ref[0].astype(BF16)) + b_ref[0]


def _ada(cc, w_ada, b_ada):
    depth, d, six_d = w_ada.shape
    rows = cc.shape[0]
    tn = six_d // 4
    return pl.pallas_call(
        _ada_body,
        out_shape=jax.ShapeDtypeStruct((depth, rows, six_d), F32),
        grid=(depth, six_d // tn),
        in_specs=[
            pl.BlockSpec((rows, d), lambda l, j: (0, 0)),
            pl.BlockSpec((1, d, tn), lambda l, j: (l, 0, j)),
            pl.BlockSpec((1, 1, tn), lambda l, j: (l, 0, j)),
        ],
        out_specs=pl.BlockSpec((1, rows, tn), lambda l, j: (l, 0, j)),
        compiler_params=_params("arbitrary", "arbitrary"),
        name="ada",
    )(cc, w_ada, b_ada.reshape(depth, 1, six_d))


def _rope(x, cos, sin, d):
    lane = lax.broadcasted_iota(jnp.int32, (x.shape[0], LANES), 1)
    first = (lane // d) % 2 == 0
    outs = []
    for j in range(x.shape[1] // LANES):
        xs = x[:, j * LANES:(j + 1) * LANES]
        partner = jnp.where(first, pltpu.roll(xs, LANES - d, 1), pltpu.roll(xs, d, 1))
        outs.append(xs * cos + partner * sin)
    return outs[0] if len(outs) == 1 else jnp.concatenate(outs, axis=1)


def _rms(x, gain):
    return x * lax.rsqrt(jnp.mean(x * x, axis=-1, keepdims=True) + RMS_EPS) * gain


def _inproj_body(x_ref, sc_ref, sh_ref, w_ref, wuq_ref, wuk_ref, bd_ref,
                 gq_ref, gkv_ref, gc_ref, cos_a_ref, sin_a_ref, cos_b_ref, sin_b_ref,
                 qa_ref, ka_ref, va_ref, qb_ref, kb_ref, vb_ref, qc_ref, kc_ref, vc_ref):
    h = (x_ref[0] * (1.0 + sc_ref[0]) + sh_ref[0]).astype(BF16)
    cos_a, sin_a = cos_a_ref[...], sin_a_ref[...]
    cos_b, sin_b = cos_b_ref[...], sin_b_ref[...]
    aq = A_HEADS * HEAD_DIM
    cq = C_HEADS * HEAD_DIM
    proj = _dot(h, w_ref[...])
    o_c, o_v, o_b = aq + LANES, aq + cq + 2 * LANES, aq + cq + 4 * LANES

    r = _rope(proj[:, :o_c], cos_a, sin_a, HEAD_DIM // 4)
    qa_ref[0] = (r[:, :aq] * A_SCALE).astype(BF16)
    ka_ref[0] = r[:, aq:].astype(BF16)

    r = proj[:, o_c:o_v]
    sq = r * r
    sq_hi = sq.astype(BF16)
    sq_lo = (sq - sq_hi.astype(F32)).astype(BF16)
    ms = _dot(sq_hi, bd_ref[...]) + _dot(sq_lo, bd_ref[...])
    r = _rope(r * lax.rsqrt(ms + RMS_EPS) * gc_ref[...], cos_a, sin_a, HEAD_DIM // 4)
    qc_ref[0] = (r[:, :cq] * C_SCALE).astype(BF16)
    kc_ref[0] = r[:, cq:].astype(BF16)

    r = proj[:, o_v:o_b].T
    va_ref[0] = r[:LANES].astype(BF16)
    vc_ref[0] = r[LANES:].astype(BF16)

    r = proj[:, o_b:]
    c_q = _rms(r[:, :B_Q_LORA], gq_ref[...]).astype(BF16)
    c_kv = _rms(r[:, B_Q_LORA:B_Q_LORA + B_KV_LORA], gkv_ref[...]).astype(BF16)
    k_rope = _rope(r[:, B_Q_LORA + B_KV_LORA:], cos_b, sin_b, B_ROPE // 4)
    q = _rope(_dot(c_q, wuq_ref[...]), cos_b, sin_b, B_ROPE // 4)
    qb_ref[0] = (q * B_SCALE).astype(BF16)
    kv = _dot(c_kv, wuk_ref[...])
    kw = B_HEADS * B_HEAD_PAD
    kb_ref[0] = (kv[:, :kw] + jnp.concatenate([k_rope] * B_HEADS, axis=1)).astype(BF16)
    vb_ref[0] = kv[:, kw:].T.astype(BF16)


def _inproj(x, sc, sh, wts, tabs, tm):
    b, n, d = x.shape
    tm = min(tm, n)
    cos_a, sin_a, cos_b, sin_b = tabs
    full = lambda a: pl.BlockSpec(a.shape, lambda i, bb: (0,) * a.ndim, pipeline_mode=pl.Buffered(1))
    tok = lambda w: pl.BlockSpec((1, tm, w), lambda i, bb: (bb, i, 0))
    mod = pl.BlockSpec((1, 1, d), lambda i, bb: (bb, 0, 0))
    tab = pl.BlockSpec((tm, LANES), lambda i, bb: (i, 0))
    widths = (A_HEADS * HEAD_DIM, LANES, -LANES, B_HEADS * B_HEAD_PAD, B_HEADS * B_HEAD_PAD, -B_HEADS * B_V,
              C_HEADS * HEAD_DIM, LANES, -LANES)
    tok_t = lambda w: pl.BlockSpec((1, w, tm), lambda i, bb: (bb, 0, i))
    return pl.pallas_call(
        _inproj_body,
        out_shape=[jax.ShapeDtypeStruct((b, n, w) if w > 0 else (b, -w, n), BF16) for w in widths],
        grid=(n // tm, b),
        in_specs=[tok(d), mod, mod] + [full(a) for a in wts] + [tab] * 4,
        out_specs=[tok(w) if w > 0 else tok_t(-w) for w in widths],
        compiler_params=_params("arbitrary", "arbitrary"),
        name="inproj",
    )(x, sc, sh, *wts, cos_a, sin_a, cos_b, sin_b)


def _attend_t(q, segs, sink):
    scores = []
    for k, _, valid in segs:
        st = _dot_nt(k, q)
        scores.append(st if valid is None else jnp.where(valid, st, NEG_BIG))
    m = scores[0].max(axis=0, keepdims=True)
    for st in scores[1:]:
        m = jnp.maximum(m, st.max(axis=0, keepdims=True))
    if sink is not None:
        m = jnp.maximum(m, sink)
    out_t = 0.0
    for st, (_, vt, _) in zip(scores, segs):
        vt1 = jnp.concatenate([vt, jnp.ones((BF16_SUBLANES, vt.shape[1]), BF16)], axis=0)
        out_t = out_t + _dot(vt1, jnp.exp2(st - m).astype(BF16))
    den = out_t[LANES:LANES + 1]
    if sink is not None:
        den = den + jnp.exp2(sink - m)
    return (out_t[:LANES] / den).T


def _attn_plan(hq, hkv, dq):
    if dq == LANES:
        return [([h], None, h * LANES, (h // 2) * LANES, [h], h % 2, [h // 2]) for h in range(hq)]
    rep = hq // hkv
    assert hkv == 2 and dq == LANES // 2
    return [(list(range(rep)), g, 0, 0, [g * rep + r for r in range(rep)], g, list(range(rep))) for g in range(hkv)]


def _attn_units(q, segs_of, sink_ref, plan):
    tq = q.shape[0]
    low_half = lax.broadcasted_iota(jnp.int32, (tq, LANES), 1) < LANES // 2
    pieces = {}
    for tiles, keep, k0, v0, heads, out_half, out_tiles in plan:
        qs = []
        for t in tiles:
            qt = q[:, t * LANES:(t + 1) * LANES]
            if keep is not None:
                qt = jnp.where(low_half == (keep == 0), qt, jnp.zeros_like(qt))
            qs.append(qt)
        qg = qs[0] if len(qs) == 1 else jnp.concatenate(qs, axis=0)
        sink = None
        if sink_ref is not None:
            sink = jnp.concatenate([jnp.full((1, tq), sink_ref[hd] * LOG2E, F32) for hd in heads], axis=1)
        og = _attend_t(qg, segs_of(k0, v0), sink)
        for j, t in enumerate(out_tiles):
            pieces[(t, out_half)] = og[j * tq:(j + 1) * tq]
    n_out = 1 + max(t for t, _ in pieces)
    outs = [jnp.where(low_half, pieces[(t, 0)], pieces[(t, 1)]) for t in range(n_out)]
    return (outs[0] if n_out == 1 else jnp.concatenate(outs, axis=1)).astype(BF16)


def _attn_body(*refs, n_seg, plan, use_sink):
    q_ref = refs[0]
    kv_refs = refs[1:1 + 2 * n_seg]
    sink_ref = refs[1 + 2 * n_seg] if use_sink else None
    o_ref = refs[-1]

    def segs_of(k0, v0):
        return [(kv_refs[2 * s][0, :, k0:k0 + LANES], kv_refs[2 * s + 1][0, v0:v0 + LANES, :], None)
                for s in range(n_seg)]

    o_ref[0] = _attn_units(q_ref[0], segs_of, sink_ref, plan)


def _attn(q, segs, sink, *, hq, hkv, dq, dv, tq, name):
    b, n, _ = q.shape
    tq = min(tq, n)
    in_specs = [pl.BlockSpec((1, tq, hq * dq), lambda bb, i: (bb, i, 0))]
    args = [q]
    for k, vt in segs:
        in_specs.append(pl.BlockSpec((1,) + k.shape[1:], lambda bb, i: (bb, 0, 0)))
        in_specs.append(pl.BlockSpec((1,) + vt.shape[1:], lambda bb, i: (bb, 0, 0)))
        args += [k, vt]
    if sink is not None:
        in_specs.append(pl.BlockSpec(memory_space=pltpu.SMEM))
        args.append(sink)
    body = functools.partial(_attn_body, n_seg=len(segs), plan=_attn_plan(hq, hkv, dq), use_sink=sink is not None)
    return pl.pallas_call(
        body,
        out_shape=jax.ShapeDtypeStruct((b, n, hq * dv), BF16),
        grid=(b, n // tq),
        in_specs=in_specs,
        out_specs=pl.BlockSpec((1, tq, hq * dv), lambda bb, i: (bb, i, 0)),
        compiler_params=_params("arbitrary", "arbitrary"),
        name=name,
    )(*args)


def _attn_a_body(*refs, seq, tq, nkb):
    q_ref = refs[0]
    k_refs, v_refs = refs[1:1 + nkb], refs[1 + nkb:1 + 2 * nkb]
    ck_ref, cvt_ref, sink_ref, o_ref = refs[1 + 2 * nkb:]
    start = pl.program_id(1) * tq
    span = nkb * WINDOW
    k_win = jnp.concatenate([r[0] for r in k_refs], axis=0)
    vt_win = jnp.concatenate([r[0] for r in v_refs], axis=1)
    kpos = start - WINDOW + lax.broadcasted_iota(jnp.int32, (span, tq), 0)
    qpos = start + lax.broadcasted_iota(jnp.int32, (span, tq), 1)
    valid = (kpos >= 0) & (kpos < seq) & (jnp.abs(qpos - kpos) <= WINDOW)
    valid = jnp.concatenate([valid] * (A_HEADS // A_KV_HEADS), axis=1)

    def segs_of(k0, v0):
        return [(k_win, vt_win, valid), (ck_ref[0], cvt_ref[0], None)]

    o_ref[0] = _attn_units(q_ref[0], segs_of, sink_ref, _attn_plan(A_HEADS, A_KV_HEADS, HEAD_DIM))


def _attn_a(q, k, vt, ck, cvt, sink, tq):
    b, n, w = q.shape
    assert WINDOW == LANES and tq % WINDOW == 0 and n % tq == 0
    per = tq // WINDOW
    nkb = per + 2
    last = n // WINDOW - 1
    blk = lambda j: (lambda i: jnp.clip(i * per + j - 1, 0, last))
    k_blk = lambda f: pl.BlockSpec((1, WINDOW, LANES), lambda bb, i: (bb, f(i), 0))
    v_blk = lambda f: pl.BlockSpec((1, LANES, WINDOW), lambda bb, i: (bb, 0, f(i)))
    whole = lambda a: pl.BlockSpec((1,) + a.shape[1:], lambda bb, i: (bb, 0, 0))
    return pl.pallas_call(
        functools.partial(_attn_a_body, seq=n, tq=tq, nkb=nkb),
        out_shape=jax.ShapeDtypeStruct((b, n, w), BF16),
        grid=(b, n // tq),
        in_specs=[pl.BlockSpec((1, tq, w), lambda bb, i: (bb, i, 0))]
                 + [k_blk(blk(j)) for j in range(nkb)] + [v_blk(blk(j)) for j in range(nkb)]
                 + [whole(ck), whole(cvt), pl.BlockSpec(memory_space=pltpu.SMEM)],
        out_specs=pl.BlockSpec((1, tq, w), lambda bb, i: (bb, i, 0)),
        compiler_params=_params("arbitrary", "arbitrary"),
        name="attn_a_window",
    )(q, *([k] * nkb), *([vt] * nkb), ck, cvt, sink)


def _sigmoid(z):
    return 1.0 / (1.0 + jnp.exp(-z))


def _layer_norm(z, g, b):
    mu = jnp.mean(z, axis=-1, keepdims=True)
    zc = z - mu
    var = jnp.mean(zc * zc, axis=-1, keepdims=True)
    return zc * lax.rsqrt(var + LN_EPS) * g + b


def _merge_body(oa_ref, ob_ref, oc_ref, x_ref, sc1_ref, sh1_ref, g1_ref, sc2_ref, sh2_ref, lng_ref, lnb_ref,
                wg_ref, wa_ref, wb_ref, wc_ref, wo_ref, wr_ref, x1_ref, h2_ref, aff_ref, *, alpha):
    d = x_ref.shape[-1]
    x = x_ref[0]
    h1 = (x * (1.0 + sc1_ref[0]) + sh1_ref[0]).astype(BF16)
    merged = 0.0
    for j, (o_ref, w_ref) in enumerate(((oa_ref, wa_ref), (ob_ref, wb_ref), (oc_ref, wc_ref))):
        gate = _sigmoid(_dot(h1, wg_ref[:, j * d:(j + 1) * d]))
        merged = merged + gate * _dot(o_ref[0], w_ref[...])
    y = _dot(merged.astype(BF16), wo_ref[...])
    x1 = _layer_norm(alpha * x + g1_ref[0] * y, lng_ref[...], lnb_ref[...])
    x1_ref[0] = x1
    h2 = (x1 * (1.0 + sc2_ref[0]) + sh2_ref[0]).astype(BF16)
    h2_ref[0] = _pack_pairs(h2)
    logits = _dot(h2, wr_ref[...])
    lane = lax.broadcasted_iota(jnp.int32, logits.shape, 1)
    logits = jnp.where(lane < N_EXPERTS, logits, NEG_BIG)
    e = jnp.exp(logits - logits.max(axis=-1, keepdims=True))
    aff_ref[0] = e / e.sum(axis=-1, keepdims=True)


def _merge(oa, ob, oc, x, sc1, sh1, g1, sc2, sh2, lng, lnb, wg, wa, wb, wc, wo, wr, alpha, tm):
    b, n, d = x.shape
    tm = min(tm, n)
    tok = lambda w: pl.BlockSpec((1, tm, w), lambda bb, i: (bb, i, 0))
    mod = pl.BlockSpec((1, 1, d), lambda bb, i: (bb, 0, 0))
    full = lambda a: pl.BlockSpec(a.shape, lambda bb, i: (0,) * a.ndim, pipeline_mode=pl.Buffered(1))
    return pl.pallas_call(
        functools.partial(_merge_body, alpha=alpha),
        out_shape=[jax.ShapeDtypeStruct((b, n, d), F32), jax.ShapeDtypeStruct((b, n, d // 2), jnp.uint32),
                   jax.ShapeDtypeStruct((b, n, LANES), F32)],
        grid=(b, n // tm),
        in_specs=[tok(oa.shape[2]), tok(ob.shape[2]), tok(oc.shape[2]), tok(d), mod, mod, mod, mod, mod,
                  full(lng), full(lnb), full(wg), full(wa), full(wb), full(wc), full(wo), full(wr)],
        out_specs=[tok(d), tok(d // 2), tok(LANES)],
        compiler_params=_params("arbitrary", "arbitrary"),
        name="merge",
    )(oa, ob, oc, x, sc1, sh1, g1, sc2, sh2, lng, lnb, wg, wa, wb, wc, wo, wr)


def _route_body(aff_ref, tri_ref, slot_ref, *idx_refs, cap):
    bits = lax.bitcast_convert_type(aff_ref[...], jnp.int32)
    count = lambda m: jnp.sum(jnp.where(m, 1.0, 0.0), axis=1, keepdims=True)
    thr = jnp.zeros((bits.shape[0], 1), jnp.int32)
    for bit in range(30, -1, -1):
        cand = thr | (1 << bit)
        thr = jnp.where(count(bits >= cand) >= cap, cand, thr)
    above = bits > thr
    tie = bits == thr
    need = cap - count(above)
    tri = tri_ref[...]
    tie_rank = _dot(jnp.where(tie, 1.0, 0.0).astype(BF16), tri)
    sel = above | (tie & (tie_rank < need))
    pos = _dot(jnp.where(sel, 1.0, 0.0).astype(BF16), tri)
    slot_ref[...] = jnp.where(sel, pos.astype(jnp.int32), -1)
    if not idx_refs:
        return
    idx_ref, cum_ref = idx_refs
    rows, n = bits.shape
    cum_ref[...] = pos + jnp.where(sel, 1.0, 0.0)
    slot_id = lax.broadcasted_iota(jnp.int32, (cap, n), 0).astype(F32)
    lane = lax.broadcasted_iota(jnp.int32, (cap, LANES), 1)
    for blk in range(rows // LANES):
        def one_row(r, tile, blk=blk):
            below = cum_ref[pl.ds(blk * LANES + r, 1), :] <= slot_id
            return jnp.where(lane == r, jnp.sum(jnp.where(below, 1.0, 0.0), axis=1, keepdims=True), tile)

        tile = lax.fori_loop(0, LANES, one_row, jnp.zeros((cap, LANES), F32))
        idx_ref[blk * LANES:(blk + 1) * LANES, :] = tile.T.astype(jnp.int32)


def _route(aff_t, cap, with_idx):
    rows, n = aff_t.shape
    tri = jnp.triu(jnp.ones((n, n), BF16), k=1)
    out_shape = [jax.ShapeDtypeStruct((rows, n), jnp.int32)]
    if with_idx:
        out_shape.append(jax.ShapeDtypeStruct((rows, cap), jnp.int32))
    return pl.pallas_call(
        functools.partial(_route_body, cap=cap),
        out_shape=out_shape,
        scratch_shapes=[pltpu.VMEM((rows, n), F32)] if with_idx else [],
        compiler_params=pltpu.CompilerParams(vmem_limit_bytes=VMEM_LIMIT),
        name="route",
    )(aff_t, tri)


def _pack_pairs(h):
    half = h.shape[1] // 2
    lo = lax.bitcast_convert_type(h[:, :half].astype(F32), jnp.uint32) >> 16
    hi = lax.bitcast_convert_type(h[:, half:].astype(F32), jnp.uint32) & jnp.uint32(0xFFFF0000)
    return hi | lo


def _unpack_pairs(p):
    lo = lax.bitcast_convert_type(p << 16, F32).astype(BF16)
    hi = lax.bitcast_convert_type(p & jnp.uint32(0xFFFF0000), F32).astype(BF16)
    return jnp.concatenate([lo, hi], axis=1)


def _experts_body(idx_ref, h_ref, slot_row_ref, aff_row_ref, slot_tok_ref, wg_ref, wu_ref, wd_ref, o_ref, *scratch,
                  cap, row_gather):
    e = pl.program_id(1)
    last = pl.num_programs(1) - 1
    group, n, _ = h_ref.shape
    picks = [slot_row_ref[s, 0] == lax.broadcasted_iota(jnp.int32, (cap, n), 0) for s in range(group)]

    def copy_rows(expert, dst_ref):
        base = (pl.program_id(0) * (last + 1) + expert) * cap
        for j in range(cap):
            dst_ref[j:j + 1, :] = h_ref[0, pl.ds(idx_ref[base + j], 1), :]

    def ffn_and_scatter(xs):
        aff = [jnp.sum(jnp.where(pick, aff_row_ref[s, 0], 0.0), axis=1, keepdims=True) for s, pick in enumerate(picks)]
        aff = aff[0] if group == 1 else jnp.concatenate(aff, axis=0)
        hid = _dot(xs, wg_ref[0, 0].astype(BF16))
        hid = hid * _sigmoid(hid) * _dot(xs, wu_ref[0, 0].astype(BF16))
        ys = (_dot(hid.astype(BF16), wd_ref[0, 0].astype(BF16)) * aff).astype(BF16)
        is_e = lax.broadcasted_iota(jnp.int32, (n, LANES), 1) == e
        for s in range(group):
            slot_col = jnp.sum(jnp.where(is_e, slot_tok_ref[s], 0.0), axis=1, keepdims=True)
            put = jnp.where(slot_col == lax.broadcasted_iota(jnp.int32, (n, cap), 1).astype(F32), 1.0, 0.0)
            o_ref[s] += _dot(put.astype(BF16), ys[s * cap:(s + 1) * cap])

    @pl.when(e == 0)
    def _():
        o_ref[...] = jnp.zeros_like(o_ref)
        if row_gather:
            copy_rows(0, scratch[0])

    if not row_gather:
        xs = [_dot(jnp.where(pick, 1.0, 0.0).astype(BF16), _unpack_pairs(h_ref[s])).astype(BF16)
              for s, pick in enumerate(picks)]
        ffn_and_scatter(xs[0] if group == 1 else jnp.concatenate(xs, axis=0))
        return

    def step(cur_ref, nxt_ref):
        copy_rows(jnp.minimum(e + 1, last), nxt_ref)
        ffn_and_scatter(_unpack_pairs(cur_ref[...]))

    @pl.when(e % 2 == 0)
    def _():
        step(scratch[0], scratch[1])

    @pl.when(e % 2 == 1)
    def _():
        step(scratch[1], scratch[0])


def _experts(h2p, slot_t, idx, aff_t, wg, wu, wd, layer, cap, group):
    b, n, half = h2p.shape
    d = 2 * half
    _, ne, _, f = wg.shape
    row_gather = idx is not None
    slot_rows = slot_t.reshape(b, ne, 1, n)
    aff_rows = aff_t.reshape(b, ne, 1, n)
    slot_tok = jnp.transpose(slot_t.reshape(b, ne, n), (0, 2, 1)).astype(F32)
    slot_tok = jnp.pad(slot_tok, ((0, 0), (0, 0), (0, LANES - ne)))
    row = pl.BlockSpec((group, 1, 1, n), lambda bb, e, idx: (bb, e, 0, 0))
    return pl.pallas_call(
        functools.partial(_experts_body, cap=cap, row_gather=row_gather),
        out_shape=jax.ShapeDtypeStruct((b, n, d), F32),
        grid_spec=pltpu.PrefetchScalarGridSpec(
            num_scalar_prefetch=1,
            grid=(b // group, ne),
            in_specs=[pl.BlockSpec((group, n, half), lambda bb, e, idx: (bb, 0, 0)), row, row,
                      pl.BlockSpec((group, n, LANES), lambda bb, e, idx: (bb, 0, 0)),
                      pl.BlockSpec((1, 1, d, f), lambda bb, e, idx: (layer, e, 0, 0)),
                      pl.BlockSpec((1, 1, d, f), lambda bb, e, idx: (layer, e, 0, 0)),
                      pl.BlockSpec((1, 1, f, d), lambda bb, e, idx: (layer, e, 0, 0))],
            out_specs=pl.BlockSpec((group, n, d), lambda bb, e, idx: (bb, 0, 0)),
            scratch_shapes=[pltpu.VMEM((cap, half), jnp.uint32)] * 2 if row_gather else []),
        compiler_params=_params("arbitrary", "arbitrary"),
        name="experts",
    )(idx.reshape(-1) if row_gather else jnp.zeros((1,), jnp.int32), h2p, slot_rows, aff_rows, slot_tok, wg, wu, wd)


def _moe(h2p, aff, wg, wu, wd, layer):
    b, n, _ = h2p.shape
    cap = CAPACITY_FACTOR * n // N_EXPERTS
    group = min(b, max(1, MXU_ROWS // cap))
    while b % group:
        group -= 1
    rows = b * N_EXPERTS
    row_gather = group == 1 and rows % LANES == 0 and cap % LANES == 0
    aff_t = jnp.transpose(aff[:, :, :N_EXPERTS], (0, 2, 1)).reshape(rows, n)
    outs = _route(aff_t, cap, row_gather)
    return _experts(h2p, outs[0], outs[1] if row_gather else None, aff_t, wg, wu, wd, layer, cap, group)


def _ln2_body(x_ref, y_ref, g2_ref, lng_ref, lnb_ref, o_ref, *, alpha):
    o_ref[0] = _layer_norm(alpha * x_ref[0] + g2_ref[0] * y_ref[0], lng_ref[...], lnb_ref[...])


def _ln2(x1, y, g2, lng, lnb, alpha, tm):
    b, n, d = x1.shape
    tm = min(tm, n)
    tok = pl.BlockSpec((1, tm, d), lambda bb, i: (bb, i, 0))
    vec = pl.BlockSpec((1, d), lambda bb, i: (0, 0))
    return pl.pallas_call(
        functools.partial(_ln2_body, alpha=alpha),
        out_shape=jax.ShapeDtypeStruct((b, n, d), F32),
        grid=(b, n // tm),
        in_specs=[tok, tok, pl.BlockSpec((1, 1, d), lambda bb, i: (bb, 0, 0)), vec, vec],
        out_specs=tok,
        compiler_params=_params("arbitrary", "arbitrary"),
        name="ln2",
    )(x1, y, g2, lng, lnb)


def _rope_tables(n, enabled):
    lane = np.arange(LANES)
    if not enabled:
        one, zero = jnp.ones((n, LANES), F32), jnp.zeros((n, LANES), F32)
        return one, zero, one, zero
    t = jnp.arange(n)
    pos = jnp.stack([t // GRID_W, t % GRID_W], axis=1).astype(F32)

    def build(jj, d, active):
        freqs = ROPE_THETA ** (-jnp.arange(d, dtype=F32) / d)
        ang = pos[:, (jj // (2 * d)) % 2] * freqs[jj % d][None, :]
        sign = np.where((jj // d) % 2 == 0, -1.0, 1.0).astype(np.float32)
        cos = jnp.where(active[None, :], jnp.cos(ang), 1.0)
        sin = jnp.where(active[None, :], jnp.sin(ang) * sign[None, :], 0.0)
        return cos, sin

    cos_a, sin_a = build(lane % HEAD_DIM, HEAD_DIM // 4, np.ones(LANES, bool))
    in_rope = (lane >= B_NOPE) & (lane < B_NOPE + B_ROPE)
    cos_b, sin_b = build(np.where(in_rope, lane - B_NOPE, 0), B_ROPE // 4, in_rope)
    return cos_a, sin_a, cos_b, sin_b


def _pair_heads(w, axis, heads, groups):
    shape = w.shape
    split = shape[:axis] + (groups, heads // groups, HEAD_DIM) + shape[axis + 1:]
    return jnp.swapaxes(w.reshape(split), axis, axis + 1).reshape(shape)


def _layer_weights(w_in, b_q_norm, b_kv_norm, b_w_uq, b_w_ukv, c_q_norm, c_k_norm):
    d = w_in.shape[0]
    sizes = (A_HEADS * HEAD_DIM, A_KV_HEADS * HEAD_DIM, A_KV_HEADS * HEAD_DIM, B_Q_LORA, B_KV_LORA, B_ROPE,
             C_HEADS * HEAD_DIM, C_KV_HEADS * HEAD_DIM, C_KV_HEADS * HEAD_DIM)
    offs = np.cumsum((0,) + sizes)
    a_q, a_k, a_v, b_cq, b_ckv, b_kr, c_q, c_k, c_v = [w_in[:, offs[i]:offs[i + 1]] for i in range(9)]
    gate = w_in[:, offs[9]:]
    zeros = lambda w: jnp.zeros((d, w), w_in.dtype)
    w1 = jnp.concatenate([_pair_heads(a_q, 1, A_HEADS, A_KV_HEADS), a_k], axis=1)
    w2 = jnp.concatenate([_pair_heads(c_q, 1, C_HEADS, C_KV_HEADS), c_k], axis=1)
    w3 = jnp.concatenate([a_v, c_v], axis=1)
    w4 = jnp.concatenate([b_cq, b_ckv, zeros(B_NOPE), b_kr, zeros(B_HEAD_PAD - B_NOPE - B_ROPE)], axis=1)
    uq = b_w_uq.reshape(B_Q_LORA, B_HEADS, B_NOPE + B_ROPE)
    uq = jnp.pad(uq, ((0, 0), (0, 0), (0, B_HEAD_PAD - B_NOPE - B_ROPE))).reshape(B_Q_LORA, B_HEADS * B_HEAD_PAD)
    ukv = b_w_ukv.reshape(B_KV_LORA, B_HEADS, B_NOPE + B_V)
    uk = jnp.pad(ukv[:, :, :B_NOPE], ((0, 0), (0, 0), (0, B_HEAD_PAD - B_NOPE))).reshape(B_KV_LORA, B_HEADS * B_HEAD_PAD)
    uv = ukv[:, :, B_NOPE:].reshape(B_KV_LORA, B_HEADS * B_V)
    wuk = jnp.concatenate([uk, uv], axis=1)
    width = w2.shape[1]
    blk = np.arange(width) // HEAD_DIM
    bd = jnp.asarray((blk[:, None] == blk[None, :]).astype(np.float32) / HEAD_DIM, BF16)
    gc = jnp.concatenate([jnp.tile(c_q_norm, C_HEADS), jnp.tile(c_k_norm, C_KV_HEADS)])[None, :]
    cast = lambda a: a.astype(BF16)
    return (cast(jnp.concatenate([w1, w2, w3, w4], axis=1)), cast(uq), cast(wuk), bd,
            b_q_norm[None, :], b_kv_norm[None, :], gc), cast(gate)


def kernel(x, c, ctx, c_ctx, w_ada, b_ada, w_in, a_sink, b_q_norm, b_kv_norm, b_w_uq, b_w_ukv, c_q_norm, c_k_norm,
           w_branch, w_out, ln1_g, ln1_b, w_router, w_gate, w_up, w_down, ln2_g, ln2_b):
    bsz, n, d = x.shape
    depth = w_ada.shape[0]
    alpha = (2 * depth) ** 0.25
    aq, bo = A_HEADS * HEAD_DIM, B_HEADS * B_V
    tm = 512

    rows = -(-(bsz + 1) // 8) * 8
    cc = jnp.concatenate([c, c_ctx[None, :], jnp.zeros((rows - bsz - 1, d), c.dtype)], axis=0)
    mod = _ada(cc, w_ada, b_ada)
    tabs_lat = _rope_tables(n, True)
    tabs_ctx = _rope_tables(ctx.shape[1], False)

    b_kw = dict(hq=B_HEADS, hkv=B_HEADS, dq=B_HEAD_PAD, dv=B_V, tq=1024)
    c_kw = dict(hq=C_HEADS, hkv=C_KV_HEADS, dq=HEAD_DIM, dv=HEAD_DIM, tq=512)
    a_kw = dict(hq=A_HEADS, hkv=A_KV_HEADS, dq=HEAD_DIM, dv=HEAD_DIM, tq=256)

    cx = ctx
    for l in range(depth):
        last = l == depth - 1
        lat_mod = [m[:, None, :] for m in jnp.split(mod[l, :bsz], 6, axis=-1)]
        ctx_mod = [jnp.broadcast_to(m[None, None, :], (bsz, 1, d)) for m in jnp.split(mod[l, bsz], 6, axis=-1)]
        sh1, sc1, g1, sh2, sc2, g2 = lat_mod
        sh1c, sc1c, g1c, sh2c, sc2c, g2c = ctx_mod
        wts, w_gl = _layer_weights(w_in[l], b_q_norm[l], b_kv_norm[l], b_w_uq[l], b_w_ukv[l], c_q_norm[l], c_k_norm[l])
        wbr = w_branch[l].astype(BF16)
        merge_w = (w_gl, _pair_heads(wbr[:aq], 0, A_HEADS, A_KV_HEADS), wbr[aq:aq + bo],
                   _pair_heads(wbr[aq + bo:], 0, C_HEADS, C_KV_HEADS), w_out[l].astype(BF16),
                   jnp.pad(w_router[l], ((0, 0), (0, LANES - N_EXPERTS))).astype(BF16))
        ffn_w = (w_gate, w_up, w_down, l)
        ln1 = (ln1_g[l][None, :], ln1_b[l][None, :])
        ln2 = (ln2_g[l][None, :], ln2_b[l][None, :])

        qa, ka, va, qb, kb, vb, qc, kc, vc = _inproj(x, sc1, sh1, wts, tabs_lat, 2 * tm)
        cqa, cka, cva, cqb, ckb, cvb, cqc, ckc, cvc = _inproj(cx, sc1c, sh1c, wts, tabs_ctx, tm)

        o_a = _attn_a(qa, ka, va, cka, cva, a_sink[l], min(n, 256))
        o_b = _attn(qb, [(ckb, cvb), (kb, vb)], None, name="attn_b", **b_kw)
        o_c = _attn(qc, [(ckc, cvc), (kc, vc)], None, name="attn_c", **c_kw)
        x1, h2, aff = _merge(o_a, o_b, o_c, x, sc1, sh1, g1, sc2, sh2, *ln1, *merge_w, alpha, tm)
        if not last:
            oc_a = _attn(cqa, [(cka, cva)], a_sink[l], name="attn_a_ctx", **a_kw)
            oc_b = _attn(cqb, [(ckb, cvb)], None, name="attn_b_ctx", **b_kw)
            oc_c = _attn(cqc, [(ckc, cvc)], None, name="attn_c_ctx", **c_kw)
            cx1, hc2, affc = _merge(oc_a, oc_b, oc_c, cx, sc1c, sh1c, g1c, sc2c, sh2c, *ln1, *merge_w, alpha, tm)

        x = _ln2(x1, _moe(h2, aff, *ffn_w), g2, *ln2, alpha, tm)
        if not last:
            cx = _ln2(cx1, _moe(hc2, affc, *ffn_w), g2c, *ln2, alpha, tm)
    return x
```

```python
import functools

import jax
import jax.numpy as jnp
import numpy as np
from jax import lax
from jax.experimental import pallas as pl
from jax.experimental.pallas import tpu as pltpu

F32 = jnp.float32
BF16 = jnp.bfloat16

GRID_W = 64
HEAD_DIM = 64
A_HEADS, A_KV_HEADS = 6, 2
WINDOW = 128
B_HEADS, B_NOPE, B_ROPE, B_V = 4, 64, 32, 64
B_Q_LORA, B_KV_LORA = 256, 128
C_HEADS, C_KV_HEADS = 6, 2
N_EXPERTS = 16
CAPACITY_FACTOR = 2
ROPE_THETA = 10000.0
LN_EPS = 1e-5
RMS_EPS = 1e-6
LOG2E = float(np.log2(np.e))
A_SCALE = HEAD_DIM ** -0.5 * LOG2E
B_SCALE = (B_NOPE + B_ROPE) ** -0.5 * LOG2E
C_SCALE = HEAD_DIM ** -0.5 * LOG2E

LANES = 128
BF16_SUBLANES = 16
MXU_ROWS = 256
B_HEAD_PAD = 128
NEG_BIG = -1e30
VMEM_LIMIT = 56 * 1024 * 1024


def _params(*sem):
    return pltpu.CompilerParams(dimension_semantics=sem, vmem_limit_bytes=VMEM_LIMIT)


def _dot(a, b):
    return jnp.dot(a, b, preferred_element_type=F32)


def _dot_nt(a, b):
    return lax.dot_general(a, b, (((1,), (1,)), ((), ())), preferred_element_type=F32)


def _ada_body(c_ref, w_ref, b_ref, o_ref):
    c = c_ref[...]
    s = (c * (1.0 / (1.0 + jnp.exp(-c)))).astype(BF16)
    o_ref[0] = _dot(s, w_ref[0].astype(BF16)) + b_ref[0]


def _ada(cc, w_ada, b_ada):
    depth, d, six_d = w_ada.shape
    rows = cc.shape[0]
    tn = six_d // 4
    return pl.pallas_call(
        _ada_body,
        out_shape=jax.ShapeDtypeStruct((depth, rows, six_d), F32),
        grid=(depth, six_d // tn),
        in_specs=[
            pl.BlockSpec((rows, d), lambda l, j: (0, 0)),
            pl.BlockSpec((1, d, tn), lambda l, j: (l, 0, j)),
            pl.BlockSpec((1, 1, tn), lambda l, j: (l, 0, j)),
        ],
        out_specs=pl.BlockSpec((1, rows, tn), lambda l, j: (l, 0, j)),
        compiler_params=_params("arbitrary", "arbitrary"),
        name="ada",
    )(cc, w_ada, b_ada.reshape(depth, 1, six_d))


def _rope(x, cos, sin, d):
    lane = lax.broadcasted_iota(jnp.int32, (x.shape[0], LANES), 1)
    first = (lane // d) % 2 == 0
    outs = []
    for j in range(x.shape[1] // LANES):
        xs = x[:, j * LANES:(j + 1) * LANES]
        partner = jnp.where(first, pltpu.roll(xs, LANES - d, 1), pltpu.roll(xs, d, 1))
        outs.append(xs * cos + partner * sin)
    return outs[0] if len(outs) == 1 else jnp.concatenate(outs, axis=1)


def _rms(x, gain):
    return x * lax.rsqrt(jnp.mean(x * x, axis=-1, keepdims=True) + RMS_EPS) * gain


def _inproj_body(x_ref, sc_ref, sh_ref, w_ref, wuq_ref, wuk_ref, bd_ref,
                 gq_ref, gkv_ref, gc_ref, cos_a_ref, sin_a_ref, cos_b_ref, sin_b_ref,
                 qa_ref, ka_ref, va_ref, qb_ref, kb_ref, vb_ref, qc_ref, kc_ref, vc_ref):
    h = (x_ref[0] * (1.0 + sc_ref[0]) + sh_ref[0]).astype(BF16)
    cos_a, sin_a = cos_a_ref[...], sin_a_ref[...]
    cos_b, sin_b = cos_b_ref[...], sin_b_ref[...]
    aq = A_HEADS * HEAD_DIM
    cq = C_HEADS * HEAD_DIM
    proj = _dot(h, w_ref[...])
    o_c, o_v, o_b = aq + LANES, aq + cq + 2 * LANES, aq + cq + 4 * LANES

    r = _rope(proj[:, :o_c], cos_a, sin_a, HEAD_DIM // 4)
    qa_ref[0] = (r[:, :aq] * A_SCALE).astype(BF16)
    ka_ref[0] = r[:, aq:].astype(BF16)

    r = proj[:, o_c:o_v]
    sq = r * r
    sq_hi = sq.astype(BF16)
    sq_lo = (sq - sq_hi.astype(F32)).astype(BF16)
    ms = _dot(sq_hi, bd_ref[...]) + _dot(sq_lo, bd_ref[...])
    r = _rope(r * lax.rsqrt(ms + RMS_EPS) * gc_ref[...], cos_a, sin_a, HEAD_DIM // 4)
    qc_ref[0] = (r[:, :cq] * C_SCALE).astype(BF16)
    kc_ref[0] = r[:, cq:].astype(BF16)

    r = proj[:, o_v:o_b].T
    va_ref[0] = r[:LANES].astype(BF16)
    vc_ref[0] = r[LANES:].astype(BF16)

    r = proj[:, o_b:]
    c_q = _rms(r[:, :B_Q_LORA], gq_ref[...]).astype(BF16)
    c_kv = _rms(r[:, B_Q_LORA:B_Q_LORA + B_KV_LORA], gkv_ref[...]).astype(BF16)
    k_rope = _rope(r[:, B_Q_LORA + B_KV_LORA:], cos_b, sin_b, B_ROPE // 4)
    q = _rope(_dot(c_q, wuq_ref[...]), cos_b, sin_b, B_ROPE // 4)
    qb_ref[0] = (q * B_SCALE).astype(BF16)
    kv = _dot(c_kv, wuk_ref[...])
    kw = B_HEADS * B_HEAD_PAD
    kb_ref[0] = (kv[:, :kw] + jnp.concatenate([k_rope] * B_HEADS, axis=1)).astype(BF16)
    vb_ref[0] = kv[:, kw:].T.astype(BF16)


def _inproj(x, sc, sh, wts, tabs, tm):
    b, n, d = x.shape
    tm = min(tm, n)
    cos_a, sin_a, cos_b, sin_b = tabs
    full = lambda a: pl.BlockSpec(a.shape, lambda i, bb: (0,) * a.ndim, pipeline_mode=pl.Buffered(1))
    tok = lambda w: pl.BlockSpec((1, tm, w), lambda i, bb: (bb, i, 0))
    mod = pl.BlockSpec((1, 1, d), lambda i, bb: (bb, 0, 0))
    tab = pl.BlockSpec((tm, LANES), lambda i, bb: (i, 0))
    widths = (A_HEADS * HEAD_DIM, LANES, -LANES, B_HEADS * B_HEAD_PAD, B_HEADS * B_HEAD_PAD, -B_HEADS * B_V,
              C_HEADS * HEAD_DIM, LANES, -LANES)
    tok_t = lambda w: pl.BlockSpec((1, w, tm), lambda i, bb: (bb, 0, i))
    return pl.pallas_call(
        _inproj_body,
        out_shape=[jax.ShapeDtypeStruct((b, n, w) if w > 0 else (b, -w, n), BF16) for w in widths],
        grid=(n // tm, b),
        in_specs=[tok(d), mod, mod] + [full(a) for a in wts] + [tab] * 4,
        out_specs=[tok(w) if w > 0 else tok_t(-w) for w in widths],
        compiler_params=_params("arbitrary", "arbitrary"),
        name="inproj",
    )(x, sc, sh, *wts, cos_a, sin_a, cos_b, sin_b)


def _attend_t(q, segs, sink):
    scores = []
    for k, _, valid in segs:
        st = _dot_nt(k, q)
        scores.append(st if valid is None else jnp.where(valid, st, NEG_BIG))
    m = scores[0].max(axis=0, keepdims=True)
    for st in scores[1:]:
        m = jnp.maximum(m, st.max(axis=0, keepdims=True))
    if sink is not None:
        m = jnp.maximum(m, sink)
    out_t = 0.0
    for st, (_, vt, _) in zip(scores, segs):
        vt1 = jnp.concatenate([vt, jnp.ones((BF16_SUBLANES, vt.shape[1]), BF16)], axis=0)
        out_t = out_t + _dot(vt1, jnp.exp2(st - m).astype(BF16))
    den = out_t[LANES:LANES + 1]
    if sink is not None:
        den = den + jnp.exp2(sink - m)
    return (out_t[:LANES] / den).T


def _attn_plan(hq, hkv, dq):
    if dq == LANES:
        return [([h], None, h * LANES, (h // 2) * LANES, [h], h % 2, [h // 2]) for h in range(hq)]
    rep = hq // hkv
    assert hkv == 2 and dq == LANES // 2
    return [(list(range(rep)), g, 0, 0, [g * rep + r for r in range(rep)], g, list(range(rep))) for g in range(hkv)]


def _attn_units(q, segs_of, sink_ref, plan):
    tq = q.shape[0]
    low_half = lax.broadcasted_iota(jnp.int32, (tq, LANES), 1) < LANES // 2
    pieces = {}
    for tiles, keep, k0, v0, heads, out_half, out_tiles in plan:
        qs = []
        for t in tiles:
            qt = q[:, t * LANES:(t + 1) * LANES]
            if keep is not None:
                qt = jnp.where(low_half == (keep == 0), qt, jnp.zeros_like(qt))
            qs.append(qt)
        qg = qs[0] if len(qs) == 1 else jnp.concatenate(qs, axis=0)
        sink = None
        if sink_ref is not None:
            sink = jnp.concatenate([jnp.full((1, tq), sink_ref[hd] * LOG2E, F32) for hd in heads], axis=1)
        og = _attend_t(qg, segs_of(k0, v0), sink)
        for j, t in enumerate(out_tiles):
            pieces[(t, out_half)] = og[j * tq:(j + 1) * tq]
    n_out = 1 + max(t for t, _ in pieces)
    outs = [jnp.where(low_half, pieces[(t, 0)], pieces[(t, 1)]) for t in range(n_out)]
    return (outs[0] if n_out == 1 else jnp.concatenate(outs, axis=1)).astype(BF16)


def _attn_body(*refs, n_seg, plan, use_sink):
    q_ref = refs[0]
    kv_refs = refs[1:1 + 2 * n_seg]
    sink_ref = refs[1 + 2 * n_seg] if use_sink else None
    o_ref = refs[-1]

    def segs_of(k0, v0):
        return [(kv_refs[2 * s][0, :, k0:k0 + LANES], kv_refs[2 * s + 1][0, v0:v0 + LANES, :], None)
                for s in range(n_seg)]

    o_ref[0] = _attn_units(q_ref[0], segs_of, sink_ref, plan)


def _attn(q, segs, sink, *, hq, hkv, dq, dv, tq, name):
    b, n, _ = q.shape
    tq = min(tq, n)
    in_specs = [pl.BlockSpec((1, tq, hq * dq), lambda bb, i: (bb, i, 0))]
    args = [q]
    for k, vt in segs:
        in_specs.append(pl.BlockSpec((1,) + k.shape[1:], lambda bb, i: (bb, 0, 0)))
        in_specs.append(pl.BlockSpec((1,) + vt.shape[1:], lambda bb, i: (bb, 0, 0)))
        args += [k, vt]
    if sink is not None:
        in_specs.append(pl.BlockSpec(memory_space=pltpu.SMEM))
        args.append(sink)
    body = functools.partial(_attn_body, n_seg=len(segs), plan=_attn_plan(hq, hkv, dq), use_sink=sink is not None)
    return pl.pallas_call(
        body,
        out_shape=jax.ShapeDtypeStruct((b, n, hq * dv), BF16),
        grid=(b, n // tq),
        in_specs=in_specs,
        out_specs=pl.BlockSpec((1, tq, hq * dv), lambda bb, i: (bb, i, 0)),
        compiler_params=_params("arbitrary", "arbitrary"),
        name=name,
    )(*args)


def _attn_a_body(*refs, seq, tq, nkb):
    q_ref = refs[0]
    k_refs, v_refs = refs[1:1 + nkb], refs[1 + nkb:1 + 2 * nkb]
    ck_ref, cvt_ref, sink_ref, o_ref = refs[1 + 2 * nkb:]
    start = pl.program_id(1) * tq
    span = nkb * WINDOW
    k_win = jnp.concatenate([r[0] for r in k_refs], axis=0)
    vt_win = jnp.concatenate([r[0] for r in v_refs], axis=1)
    kpos = start - WINDOW + lax.broadcasted_iota(jnp.int32, (span, tq), 0)
    qpos = start + lax.broadcasted_iota(jnp.int32, (span, tq), 1)
    valid = (kpos >= 0) & (kpos < seq) & (jnp.abs(qpos - kpos) <= WINDOW)
    valid = jnp.concatenate([valid] * (A_HEADS // A_KV_HEADS), axis=1)

    def segs_of(k0, v0):
        return [(k_win, vt_win, valid), (ck_ref[0], cvt_ref[0], None)]

    o_ref[0] = _attn_units(q_ref[0], segs_of, sink_ref, _attn_plan(A_HEADS, A_KV_HEADS, HEAD_DIM))


def _attn_a(q, k, vt, ck, cvt, sink, tq):
    b, n, w = q.shape
    assert WINDOW == LANES and tq % WINDOW == 0 and n % tq == 0
    per = tq // WINDOW
    nkb = per + 2
    last = n // WINDOW - 1
    blk = lambda j: (lambda i: jnp.clip(i * per + j - 1, 0, last))
    k_blk = lambda f: pl.BlockSpec((1, WINDOW, LANES), lambda bb, i: (bb, f(i), 0))
    v_blk = lambda f: pl.BlockSpec((1, LANES, WINDOW), lambda bb, i: (bb, 0, f(i)))
    whole = lambda a: pl.BlockSpec((1,) + a.shape[1:], lambda bb, i: (bb, 0, 0))
    return pl.pallas_call(
        functools.partial(_attn_a_body, seq=n, tq=tq, nkb=nkb),
        out_shape=jax.ShapeDtypeStruct((b, n, w), BF16),
        grid=(b, n // tq),
        in_specs=[pl.BlockSpec((1, tq, w), lambda bb, i: (bb, i, 0))]
                 + [k_blk(blk(j)) for j in range(nkb)] + [v_blk(blk(j)) for j in range(nkb)]
                 + [whole(ck), whole(cvt), pl.BlockSpec(memory_space=pltpu.SMEM)],
        out_specs=pl.BlockSpec((1, tq, w), lambda bb, i: (bb, i, 0)),
        compiler_params=_params("arbitrary", "arbitrary"),
        name="attn_a_window",
    )(q, *([k] * nkb), *([vt] * nkb), ck, cvt, sink)


def _sigmoid(z):
    return 1.0 / (1.0 + jnp.exp(-z))


def _layer_norm(z, g, b):
    mu = jnp.mean(z, axis=-1, keepdims=True)
    zc = z - mu
    var = jnp.mean(zc * zc, axis=-1, keepdims=True)
    return zc * lax.rsqrt(var + LN_EPS) * g + b


def _merge_body(oa_ref, ob_ref, oc_ref, x_ref, sc1_ref, sh1_ref, g1_ref, sc2_ref, sh2_ref, lng_ref, lnb_ref,
                wg_ref, wa_ref, wb_ref, wc_ref, wo_ref, wr_ref, x1_ref, h2_ref, aff_ref, *, alpha):
    d = x_ref.shape[-1]
    x = x_ref[0]
    h1 = (x * (1.0 + sc1_ref[0]) + sh1_ref[0]).astype(BF16)
    merged = 0.0
    for j, (o_ref, w_ref) in enumerate(((oa_ref, wa_ref), (ob_ref, wb_ref), (oc_ref, wc_ref))):
        gate = _sigmoid(_dot(h1, wg_ref[:, j * d:(j + 1) * d]))
        merged = merged + gate * _dot(o_ref[0], w_ref[...])
    y = _dot(merged.astype(BF16), wo_ref[...])
    x1 = _layer_norm(alpha * x + g1_ref[0] * y, lng_ref[...], lnb_ref[...])
    x1_ref[0] = x1
    h2 = (x1 * (1.0 + sc2_ref[0]) + sh2_ref[0]).astype(BF16)
    h2_ref[0] = _pack_pairs(h2)
    logits = _dot(h2, wr_ref[...])
    lane = lax.broadcasted_iota(jnp.int32, logits.shape, 1)
    logits = jnp.where(lane < N_EXPERTS, logits, NEG_BIG)
    e = jnp.exp(logits - logits.max(axis=-1, keepdims=True))
    aff_ref[0] = e / e.sum(axis=-1, keepdims=True)


def _merge(oa, ob, oc, x, sc1, sh1, g1, sc2, sh2, lng, lnb, wg, wa, wb, wc, wo, wr, alpha, tm):
    b, n, d = x.shape
    tm = min(tm, n)
    tok = lambda w: pl.BlockSpec((1, tm, w), lambda bb, i: (bb, i, 0))
    mod = pl.BlockSpec((1, 1, d), lambda bb, i: (bb, 0, 0))
    full = lambda a: pl.BlockSpec(a.shape, lambda bb, i: (0,) * a.ndim, pipeline_mode=pl.Buffered(1))
    return pl.pallas_call(
        functools.partial(_merge_body, alpha=alpha),
        out_shape=[jax.ShapeDtypeStruct((b, n, d), F32), jax.ShapeDtypeStruct((b, n, d // 2), jnp.uint32),
                   jax.ShapeDtypeStruct((b, n, LANES), F32)],
        grid=(b, n // tm),
        in_specs=[tok(oa.shape[2]), tok(ob.shape[2]), tok(oc.shape[2]), tok(d), mod, mod, mod, mod, mod,
                  full(lng), full(lnb), full(wg), full(wa), full(wb), full(wc), full(wo), full(wr)],
        out_specs=[tok(d), tok(d // 2), tok(LANES)],
        compiler_params=_params("arbitrary", "arbitrary"),
        name="merge",
    )(oa, ob, oc, x, sc1, sh1, g1, sc2, sh2, lng, lnb, wg, wa, wb, wc, wo, wr)


def _route_body(aff_ref, tri_ref, slot_ref, *idx_refs, cap):
    bits = lax.bitcast_convert_type(aff_ref[...], jnp.int32)
    count = lambda m: jnp.sum(jnp.where(m, 1.0, 0.0), axis=1, keepdims=True)
    thr = jnp.zeros((bits.shape[0], 1), jnp.int32)
    for bit in range(30, -1, -1):
        cand = thr | (1 << bit)
        thr = jnp.where(count(bits >= cand) >= cap, cand, thr)
    above = bits > thr
    tie = bits == thr
    need = cap - count(above)
    tri = tri_ref[...]
    tie_rank = _dot(jnp.where(tie, 1.0, 0.0).astype(BF16), tri)
    sel = above | (tie & (tie_rank < need))
    pos = _dot(jnp.where(sel, 1.0, 0.0).astype(BF16), tri)
    slot_ref[...] = jnp.where(sel, pos.astype(jnp.int32), -1)
    if not idx_refs:
        return
    idx_ref, cum_ref = idx_refs
    rows, n = bits.shape
    cum_ref[...] = pos + jnp.where(sel, 1.0, 0.0)
    slot_id = lax.broadcasted_iota(jnp.int32, (cap, n), 0).astype(F32)
    lane = lax.broadcasted_iota(jnp.int32, (cap, LANES), 1)
    for blk in range(rows // LANES):
        def one_row(r, tile, blk=blk):
            below = cum_ref[pl.ds(blk * LANES + r, 1), :] <= slot_id
            return jnp.where(lane == r, jnp.sum(jnp.where(below, 1.0, 0.0), axis=1, keepdims=True), tile)

        tile = lax.fori_loop(0, LANES, one_row, jnp.zeros((cap, LANES), F32))
        idx_ref[blk * LANES:(blk + 1) * LANES, :] = tile.T.astype(jnp.int32)


def _route(aff_t, cap, with_idx):
    rows, n = aff_t.shape
    tri = jnp.triu(jnp.ones((n, n), BF16), k=1)
    out_shape = [jax.ShapeDtypeStruct((rows, n), jnp.int32)]
    if with_idx:
        out_shape.append(jax.ShapeDtypeStruct((rows, cap), jnp.int32))
    return pl.pallas_call(
        functools.partial(_route_body, cap=cap),
        out_shape=out_shape,
        scratch_shapes=[pltpu.VMEM((rows, n), F32)] if with_idx else [],
        compiler_params=pltpu.CompilerParams(vmem_limit_bytes=VMEM_LIMIT),
        name="route",
    )(aff_t, tri)


def _pack_pairs(h):
    half = h.shape[1] // 2
    lo = lax.bitcast_convert_type(h[:, :half].astype(F32), jnp.uint32) >> 16
    hi = lax.bitcast_convert_type(h[:, half:].astype(F32), jnp.uint32) & jnp.uint32(0xFFFF0000)
    return hi | lo


def _unpack_pairs(p):
    lo = lax.bitcast_convert_type(p << 16, F32).astype(BF16)
    hi = lax.bitcast_convert_type(p & jnp.uint32(0xFFFF0000), F32).astype(BF16)
    return jnp.concatenate([lo, hi], axis=1)


def _experts_body(idx_ref, h_ref, slot_row_ref, aff_row_ref, slot_tok_ref, wg_ref, wu_ref, wd_ref, o_ref, *scratch,
                  cap, row_gather):
    e = pl.program_id(1)
    last = pl.num_programs(1) - 1
    group, n, _ = h_ref.shape
    picks = [slot_row_ref[s, 0] == lax.broadcasted_iota(jnp.int32, (cap, n), 0) for s in range(group)]

    def copy_rows(expert, dst_ref):
        base = (pl.program_id(0) * (last + 1) + expert) * cap
        for j in range(cap):
            dst_ref[j:j + 1, :] = h_ref[0, pl.ds(idx_ref[base + j], 1), :]

    def ffn_and_scatter(xs):
        aff = [jnp.sum(jnp.where(pick, aff_row_ref[s, 0], 0.0), axis=1, keepdims=True) for s, pick in enumerate(picks)]
        aff = aff[0] if group == 1 else jnp.concatenate(aff, axis=0)
        hid = _dot(xs, wg_ref[0, 0])
        hid = hid * _sigmoid(hid) * _dot(xs, wu_ref[0, 0])
        ys = (_dot(hid.astype(BF16), wd_ref[0, 0]) * aff).astype(BF16)
        is_e = lax.broadcasted_iota(jnp.int32, (n, LANES), 1) == e
        for s in range(group):
            slot_col = jnp.sum(jnp.where(is_e, slot_tok_ref[s], 0.0), axis=1, keepdims=True)
            put = jnp.where(slot_col == lax.broadcasted_iota(jnp.int32, (n, cap), 1).astype(F32), 1.0, 0.0)
            o_ref[s] += _dot(put.astype(BF16), ys[s * cap:(s + 1) * cap])

    @pl.when(e == 0)
    def _():
        o_ref[...] = jnp.zeros_like(o_ref)
        if row_gather:
            copy_rows(0, scratch[0])

    if not row_gather:
        xs = [_dot(jnp.where(pick, 1.0, 0.0).astype(BF16), _unpack_pairs(h_ref[s])).astype(BF16)
              for s, pick in enumerate(picks)]
        ffn_and_scatter(xs[0] if group == 1 else jnp.concatenate(xs, axis=0))
        return

    def step(cur_ref, nxt_ref):
        copy_rows(jnp.minimum(e + 1, last), nxt_ref)
        ffn_and_scatter(_unpack_pairs(cur_ref[...]))

    @pl.when(e % 2 == 0)
    def _():
        step(scratch[0], scratch[1])

    @pl.when(e % 2 == 1)
    def _():
        step(scratch[1], scratch[0])


def _experts(h2p, slot_t, idx, aff_t, wg, wu, wd, layer, cap, group):
    b, n, half = h2p.shape
    d = 2 * half
    _, ne, _, f = wg.shape
    row_gather = idx is not None
    slot_rows = slot_t.reshape(b, ne, 1, n)
    aff_rows = aff_t.reshape(b, ne, 1, n)
    slot_tok = jnp.transpose(slot_t.reshape(b, ne, n), (0, 2, 1)).astype(F32)
    slot_tok = jnp.pad(slot_tok, ((0, 0), (0, 0), (0, LANES - ne)))
    row = pl.BlockSpec((group, 1, 1, n), lambda bb, e, idx: (bb, e, 0, 0))
    return pl.pallas_call(
        functools.partial(_experts_body, cap=cap, row_gather=row_gather),
        out_shape=jax.ShapeDtypeStruct((b, n, d), F32),
        grid_spec=pltpu.PrefetchScalarGridSpec(
            num_scalar_prefetch=1,
            grid=(b // group, ne),
            in_specs=[pl.BlockSpec((group, n, half), lambda bb, e, idx: (bb, 0, 0)), row, row,
                      pl.BlockSpec((group, n, LANES), lambda bb, e, idx: (bb, 0, 0)),
                      pl.BlockSpec((1, 1, d, f), lambda bb, e, idx: (layer, e, 0, 0)),
                      pl.BlockSpec((1, 1, d, f), lambda bb, e, idx: (layer, e, 0, 0)),
                      pl.BlockSpec((1, 1, f, d), lambda bb, e, idx: (layer, e, 0, 0))],
            out_specs=pl.BlockSpec((group, n, d), lambda bb, e, idx: (bb, 0, 0)),
            scratch_shapes=[pltpu.VMEM((cap, half), jnp.uint32)] * 2 if row_gather else []),
        compiler_params=_params("arbitrary", "arbitrary"),
        name="experts",
    )(idx.reshape(-1) if row_gather else jnp.zeros((1,), jnp.int32), h2p, slot_rows, aff_rows, slot_tok, wg, wu, wd)


def _moe(h2p, aff, wg, wu, wd, layer):
    b, n, _ = h2p.shape
    cap = CAPACITY_FACTOR * n // N_EXPERTS
    group = min(b, max(1, MXU_ROWS // cap))
    while b % group:
        group -= 1
    rows = b * N_EXPERTS
    row_gather = group == 1 and rows % LANES == 0 and cap % LANES == 0
    aff_t = jnp.transpose(aff[:, :, :N_EXPERTS], (0, 2, 1)).reshape(rows, n)
    outs = _route(aff_t, cap, row_gather)
    return _experts(h2p, outs[0], outs[1] if row_gather else None, aff_t, wg, wu, wd, layer, cap, group)


def _ln2_body(x_ref, y_ref, g2_ref, lng_ref, lnb_ref, o_ref, *, alpha):
    o_ref[0] = _layer_norm(alpha * x_ref[0] + g2_ref[0] * y_ref[0], lng_ref[...], lnb_ref[...])


def _ln2(x1, y, g2, lng, lnb, alpha, tm):
    b, n, d = x1.shape
    tm = min(tm, n)
    tok = pl.BlockSpec((1, tm, d), lambda bb, i: (bb, i, 0))
    vec = pl.BlockSpec((1, d), lambda bb, i: (0, 0))
    return pl.pallas_call(
        functools.partial(_ln2_body, alpha=alpha),
        out_shape=jax.ShapeDtypeStruct((b, n, d), F32),
        grid=(b, n // tm),
        in_specs=[tok, tok, pl.BlockSpec((1, 1, d), lambda bb, i: (bb, 0, 0)), vec, vec],
        out_specs=tok,
        compiler_params=_params("arbitrary", "arbitrary"),
        name="ln2",
    )(x1, y, g2, lng, lnb)


def _rope_tables(n, enabled):
    lane = np.arange(LANES)
    if not enabled:
        one, zero = jnp.ones((n, LANES), F32), jnp.zeros((n, LANES), F32)
        return one, zero, one, zero
    t = jnp.arange(n)
    pos = jnp.stack([t // GRID_W, t % GRID_W], axis=1).astype(F32)

    def build(jj, d, active):
        freqs = ROPE_THETA ** (-jnp.arange(d, dtype=F32) / d)
        ang = pos[:, (jj // (2 * d)) % 2] * freqs[jj % d][None, :]
        sign = np.where((jj // d) % 2 == 0, -1.0, 1.0).astype(np.float32)
        cos = jnp.where(active[None, :], jnp.cos(ang), 1.0)
        sin = jnp.where(active[None, :], jnp.sin(ang) * sign[None, :], 0.0)
        return cos, sin

    cos_a, sin_a = build(lane % HEAD_DIM, HEAD_DIM // 4, np.ones(LANES, bool))
    in_rope = (lane >= B_NOPE) & (lane < B_NOPE + B_ROPE)
    cos_b, sin_b = build(np.where(in_rope, lane - B_NOPE, 0), B_ROPE // 4, in_rope)
    return cos_a, sin_a, cos_b, sin_b


def _pair_heads(w, axis, heads, groups):
    shape = w.shape
    split = shape[:axis] + (groups, heads // groups, HEAD_DIM) + shape[axis + 1:]
    return jnp.swapaxes(w.reshape(split), axis, axis + 1).reshape(shape)


def _layer_weights(w_in, b_q_norm, b_kv_norm, b_w_uq, b_w_ukv, c_q_norm, c_k_norm):
    d = w_in.shape[0]
    sizes = (A_HEADS * HEAD_DIM, A_KV_HEADS * HEAD_DIM, A_KV_HEADS * HEAD_DIM, B_Q_LORA, B_KV_LORA, B_ROPE,
             C_HEADS * HEAD_DIM, C_KV_HEADS * HEAD_DIM, C_KV_HEADS * HEAD_DIM)
    offs = np.cumsum((0,) + sizes)
    a_q, a_k, a_v, b_cq, b_ckv, b_kr, c_q, c_k, c_v = [w_in[:, offs[i]:offs[i + 1]] for i in range(9)]
    gate = w_in[:, offs[9]:]
    zeros = lambda w: jnp.zeros((d, w), w_in.dtype)
    w1 = jnp.concatenate([_pair_heads(a_q, 1, A_HEADS, A_KV_HEADS), a_k], axis=1)
    w2 = jnp.concatenate([_pair_heads(c_q, 1, C_HEADS, C_KV_HEADS), c_k], axis=1)
    w3 = jnp.concatenate([a_v, c_v], axis=1)
    w4 = jnp.concatenate([b_cq, b_ckv, zeros(B_NOPE), b_kr, zeros(B_HEAD_PAD - B_NOPE - B_ROPE)], axis=1)
    uq = b_w_uq.reshape(B_Q_LORA, B_HEADS, B_NOPE + B_ROPE)
    uq = jnp.pad(uq, ((0, 0), (0, 0), (0, B_HEAD_PAD - B_NOPE - B_ROPE))).reshape(B_Q_LORA, B_HEADS * B_HEAD_PAD)
    ukv = b_w_ukv.reshape(B_KV_LORA, B_HEADS, B_NOPE + B_V)
    uk = jnp.pad(ukv[:, :, :B_NOPE], ((0, 0), (0, 0), (0, B_HEAD_PAD - B_NOPE))).reshape(B_KV_LORA, B_HEADS * B_HEAD_PAD)
    uv = ukv[:, :, B_NOPE:].reshape(B_KV_LORA, B_HEADS * B_V)
    wuk = jnp.concatenate([uk, uv], axis=1)
    width = w2.shape[1]
    blk = np.arange(width) // HEAD_DIM
    bd = jnp.asarray((blk[:, None] == blk[None, :]).astype(np.float32) / HEAD_DIM, BF16)
    gc = jnp.concatenate([jnp.tile(c_q_norm, C_HEADS), jnp.tile(c_k_norm, C_KV_HEADS)])[None, :]
    cast = lambda a: a.astype(BF16)
    return (cast(jnp.concatenate([w1, w2, w3, w4], axis=1)), cast(uq), cast(wuk), bd,
            b_q_norm[None, :], b_kv_norm[None, :], gc), cast(gate)


def kernel(x, c, ctx, c_ctx, w_ada, b_ada, w_in, a_sink, b_q_norm, b_kv_norm, b_w_uq, b_w_ukv, c_q_norm, c_k_norm,
           w_branch, w_out, ln1_g, ln1_b, w_router, w_gate, w_up, w_down, ln2_g, ln2_b):
    bsz, n, d = x.shape
    depth = w_ada.shape[0]
    alpha = (2 * depth) ** 0.25
    aq, bo = A_HEADS * HEAD_DIM, B_HEADS * B_V
    tm = 512

    rows = -(-(bsz + 1) // 8) * 8
    cc = jnp.concatenate([c, c_ctx[None, :], jnp.zeros((rows - bsz - 1, d), c.dtype)], axis=0)
    mod = _ada(cc, w_ada, b_ada)
    tabs_lat = _rope_tables(n, True)
    tabs_ctx = _rope_tables(ctx.shape[1], False)

    b_kw = dict(hq=B_HEADS, hkv=B_HEADS, dq=B_HEAD_PAD, dv=B_V, tq=1024)
    c_kw = dict(hq=C_HEADS, hkv=C_KV_HEADS, dq=HEAD_DIM, dv=HEAD_DIM, tq=512)
    a_kw = dict(hq=A_HEADS, hkv=A_KV_HEADS, dq=HEAD_DIM, dv=HEAD_DIM, tq=256)

    ffn_bf16 = (w_gate.astype(BF16), w_up.astype(BF16), w_down.astype(BF16))
    cx = ctx
    for l in range(depth):
        last = l == depth - 1
        lat_mod = [m[:, None, :] for m in jnp.split(mod[l, :bsz], 6, axis=-1)]
        ctx_mod = [jnp.broadcast_to(m[None, None, :], (bsz, 1, d)) for m in jnp.split(mod[l, bsz], 6, axis=-1)]
        sh1, sc1, g1, sh2, sc2, g2 = lat_mod
        sh1c, sc1c, g1c, sh2c, sc2c, g2c = ctx_mod
        wts, w_gl = _layer_weights(w_in[l], b_q_norm[l], b_kv_norm[l], b_w_uq[l], b_w_ukv[l], c_q_norm[l], c_k_norm[l])
        wbr = w_branch[l].astype(BF16)
        merge_w = (w_gl, _pair_heads(wbr[:aq], 0, A_HEADS, A_KV_HEADS), wbr[aq:aq + bo],
                   _pair_heads(wbr[aq + bo:], 0, C_HEADS, C_KV_HEADS), w_out[l].astype(BF16),
                   jnp.pad(w_router[l], ((0, 0), (0, LANES - N_EXPERTS))).astype(BF16))
        ffn_w = (*ffn_bf16, l)
        ln1 = (ln1_g[l][None, :], ln1_b[l][None, :])
        ln2 = (ln2_g[l][None, :], ln2_b[l][None, :])

        qa, ka, va, qb, kb, vb, qc, kc, vc = _inproj(x, sc1, sh1, wts, tabs_lat, 2 * tm)
        cqa, cka, cva, cqb, ckb, cvb, cqc, ckc, cvc = _inproj(cx, sc1c, sh1c, wts, tabs_ctx, tm)

        o_a = _attn_a(qa, ka, va, cka, cva, a_sink[l], min(n, 256))
        o_b = _attn(qb, [(ckb, cvb), (kb, vb)], None, name="attn_b", **b_kw)
        o_c = _attn(qc, [(ckc, cvc), (kc, vc)], None, name="attn_c", **c_kw)
        x1, h2, aff = _merge(o_a, o_b, o_c, x, sc1, sh1, g1, sc2, sh2, *ln1, *merge_w, alpha, tm)
        if not last:
            oc_a = _attn(cqa, [(cka, cva)], a_sink[l], name="attn_a_ctx", **a_kw)
            oc_b = _attn(cqb, [(ckb, cvb)], None, name="attn_b_ctx", **b_kw)
            oc_c = _attn(cqc, [(ckc, cvc)], None, name="attn_c_ctx", **c_kw)
            cx1, hc2, affc = _merge(oc_a, oc_b, oc_c, cx, sc1c, sh1c, g1c, sc2c, sh2c, *ln1, *merge_w, alpha, tm)

        x = _ln2(x1, _moe(h2, aff, *ffn_w), g2, *ln2, alpha, tm)
        if not last:
            cx = _ln2(cx1, _moe(hc2, affc, *ffn_w), g2c, *ln2, alpha, tm)
    return x
```

```python
import functools

import jax
import jax.numpy as jnp
import numpy as np
from jax import lax
from jax.experimental import pallas as pl
from jax.experimental.pallas import tpu as pltpu

F32 = jnp.float32
BF16 = jnp.bfloat16

GRID_W = 64
HEAD_DIM = 64
A_HEADS, A_KV_HEADS = 6, 2
WINDOW = 128
B_HEADS, B_NOPE, B_ROPE, B_V = 4, 64, 32, 64
B_Q_LORA, B_KV_LORA = 256, 128
C_HEADS, C_KV_HEADS = 6, 2
N_EXPERTS = 16
CAPACITY_FACTOR = 2
ROPE_THETA = 10000.0
LN_EPS = 1e-5
RMS_EPS = 1e-6
LOG2E = float(np.log2(np.e))
A_SCALE = HEAD_DIM ** -0.5 * LOG2E
B_SCALE = (B_NOPE + B_ROPE) ** -0.5 * LOG2E
C_SCALE = HEAD_DIM ** -0.5 * LOG2E

LANES = 128
BF16_SUBLANES = 16
MXU_ROWS = 256
B_HEAD_PAD = 128
NEG_BIG = -1e30
VMEM_LIMIT = 56 * 1024 * 1024


def _params(*sem):
    return pltpu.CompilerParams(dimension_semantics=sem, vmem_limit_bytes=VMEM_LIMIT)


def _dot(a, b):
    return jnp.dot(a, b, preferred_element_type=F32)


def _dot_nt(a, b):
    return lax.dot_general(a, b, (((1,), (1,)), ((), ())), preferred_element_type=F32)


def _ada_body(c_ref, w_ref, b_ref, o_ref):
    c = c_ref[...]
    s = (c * (1.0 / (1.0 + jnp.exp(-c)))).astype(BF16)
    o_ref[0] = _dot(s, w_ref[0].astype(BF16)) + b_ref[0]


def _ada(cc, w_ada, b_ada):
    depth, d, six_d = w_ada.shape
    rows = cc.shape[0]
    tn = six_d // 4
    return pl.pallas_call(
        _ada_body,
        out_shape=jax.ShapeDtypeStruct((depth, rows, six_d), F32),
        grid=(depth, six_d // tn),
        in_specs=[
            pl.BlockSpec((rows, d), lambda l, j: (0, 0)),
            pl.BlockSpec((1, d, tn), lambda l, j: (l, 0, j)),
            pl.BlockSpec((1, 1, tn), lambda l, j: (l, 0, j)),
        ],
        out_specs=pl.BlockSpec((1, rows, tn), lambda l, j: (l, 0, j)),
        compiler_params=_params("arbitrary", "arbitrary"),
        name="ada",
    )(cc, w_ada, b_ada.reshape(depth, 1, six_d))


def _rope(x, cos, sin, d):
    lane = lax.broadcasted_iota(jnp.int32, (x.shape[0], LANES), 1)
    first = (lane // d) % 2 == 0
    outs = []
    for j in range(x.shape[1] // LANES):
        xs = x[:, j * LANES:(j + 1) * LANES]
        partner = jnp.where(first, pltpu.roll(xs, LANES - d, 1), pltpu.roll(xs, d, 1))
        outs.append(xs * cos + partner * sin)
    return outs[0] if len(outs) == 1 else jnp.concatenate(outs, axis=1)


def _rms(x, gain):
    return x * lax.rsqrt(jnp.mean(x * x, axis=-1, keepdims=True) + RMS_EPS) * gain


def _inproj_body(x_ref, sc_ref, sh_ref, w_ref, wuq_ref, wuk_ref, bd_ref,
                 gq_ref, gkv_ref, gc_ref, cos_a_ref, sin_a_ref, cos_b_ref, sin_b_ref,
                 qa_ref, ka_ref, va_ref, qb_ref, kb_ref, vb_ref, qc_ref, kc_ref, vc_ref):
    h = (x_ref[0] * (1.0 + sc_ref[0]) + sh_ref[0]).astype(BF16)
    cos_a, sin_a = cos_a_ref[...], sin_a_ref[...]
    cos_b, sin_b = cos_b_ref[...], sin_b_ref[...]
    aq = A_HEADS * HEAD_DIM
    cq = C_HEADS * HEAD_DIM
    proj = _dot(h, w_ref[...])
    o_c, o_v, o_b = aq + LANES, aq + cq + 2 * LANES, aq + cq + 4 * LANES

    r = _rope(proj[:, :o_c], cos_a, sin_a, HEAD_DIM // 4)
    qa_ref[0] = (r[:, :aq] * A_SCALE).astype(BF16)
    ka_ref[0] = r[:, aq:].astype(BF16)

    r = proj[:, o_c:o_v]
    sq = r * r
    sq_hi = sq.astype(BF16)
    sq_lo = (sq - sq_hi.astype(F32)).astype(BF16)
    ms = _dot(sq_hi, bd_ref[...]) + _dot(sq_lo, bd_ref[...])
    r = _rope(r * lax.rsqrt(ms + RMS_EPS) * gc_ref[...], cos_a, sin_a, HEAD_DIM // 4)
    qc_ref[0] = (r[:, :cq] * C_SCALE).astype(BF16)
    kc_ref[0] = r[:, cq:].astype(BF16)

    r = proj[:, o_v:o_b].T
    va_ref[0] = r[:LANES].astype(BF16)
    vc_ref[0] = r[LANES:].astype(BF16)

    r = proj[:, o_b:]
    c_q = _rms(r[:, :B_Q_LORA], gq_ref[...]).astype(BF16)
    c_kv = _rms(r[:, B_Q_LORA:B_Q_LORA + B_KV_LORA], gkv_ref[...]).astype(BF16)
    k_rope = _rope(r[:, B_Q_LORA + B_KV_LORA:], cos_b, sin_b, B_ROPE // 4)
    q = _rope(_dot(c_q, wuq_ref[...]), cos_b, sin_b, B_ROPE // 4)
    qb_ref[0] = (q * B_SCALE).astype(BF16)
    kv = _dot(c_kv, wuk_ref[...])
    kw = B_HEADS * B_HEAD_PAD
    kb_ref[0] = (kv[:, :kw] + jnp.concatenate([k_rope] * B_HEADS, axis=1)).astype(BF16)
    vb_ref[0] = kv[:, kw:].T.astype(BF16)


def _inproj(x, sc, sh, wts, tabs, tm):
    b, n, d = x.shape
    tm = min(tm, n)
    cos_a, sin_a, cos_b, sin_b = tabs
    full = lambda a: pl.BlockSpec(a.shape, lambda i, bb: (0,) * a.ndim, pipeline_mode=pl.Buffered(1))
    tok = lambda w: pl.BlockSpec((1, tm, w), lambda i, bb: (bb, i, 0))
    mod = pl.BlockSpec((1, 1, d), lambda i, bb: (bb, 0, 0))
    tab = pl.BlockSpec((tm, LANES), lambda i, bb: (i, 0))
    widths = (A_HEADS * HEAD_DIM, LANES, -LANES, B_HEADS * B_HEAD_PAD, B_HEADS * B_HEAD_PAD, -B_HEADS * B_V,
              C_HEADS * HEAD_DIM, LANES, -LANES)
    tok_t = lambda w: pl.BlockSpec((1, w, tm), lambda i, bb: (bb, 0, i))
    return pl.pallas_call(
        _inproj_body,
        out_shape=[jax.ShapeDtypeStruct((b, n, w) if w > 0 else (b, -w, n), BF16) for w in widths],
        grid=(n // tm, b),
        in_specs=[tok(d), mod, mod] + [full(a) for a in wts] + [tab] * 4,
        out_specs=[tok(w) if w > 0 else tok_t(-w) for w in widths],
        compiler_params=_params("arbitrary", "arbitrary"),
        name="inproj",
    )(x, sc, sh, *wts, cos_a, sin_a, cos_b, sin_b)


def _attend_t(q, segs, sink):
    scores = []
    for k, _, valid in segs:
        st = _dot_nt(k, q)
        scores.append(st if valid is None else jnp.where(valid, st, NEG_BIG))
    m = scores[0].max(axis=0, keepdims=True)
    for st in scores[1:]:
        m = jnp.maximum(m, st.max(axis=0, keepdims=True))
    if sink is not None:
        m = jnp.maximum(m, sink)
    out_t = 0.0
    for st, (_, vt, _) in zip(scores, segs):
        vt1 = jnp.concatenate([vt, jnp.ones((BF16_SUBLANES, vt.shape[1]), BF16)], axis=0)
        out_t = out_t + _dot(vt1, jnp.exp2(st - m).astype(BF16))
    den = out_t[LANES:LANES + 1]
    if sink is not None:
        den = den + jnp.exp2(sink - m)
    return (out_t[:LANES] / den).T


def _attn_plan(hq, hkv, dq):
    if dq == LANES:
        return [([h], None, h * LANES, (h // 2) * LANES, [h], h % 2, [h // 2]) for h in range(hq)]
    rep = hq // hkv
    assert hkv == 2 and dq == LANES // 2
    return [(list(range(rep)), g, 0, 0, [g * rep + r for r in range(rep)], g, list(range(rep))) for g in range(hkv)]


def _attn_units(q, segs_of, sink_ref, plan):
    tq = q.shape[0]
    low_half = lax.broadcasted_iota(jnp.int32, (tq, LANES), 1) < LANES // 2
    pieces = {}
    for tiles, keep, k0, v0, heads, out_half, out_tiles in plan:
        qs = []
        for t in tiles:
            qt = q[:, t * LANES:(t + 1) * LANES]
            if keep is not None:
                qt = jnp.where(low_half == (keep == 0), qt, jnp.zeros_like(qt))
            qs.append(qt)
        qg = qs[0] if len(qs) == 1 else jnp.concatenate(qs, axis=0)
        sink = None
        if sink_ref is not None:
            sink = jnp.concatenate([jnp.full((1, tq), sink_ref[hd] * LOG2E, F32) for hd in heads], axis=1)
        og = _attend_t(qg, segs_of(k0, v0), sink)
        for j, t in enumerate(out_tiles):
            pieces[(t, out_half)] = og[j * tq:(j + 1) * tq]
    n_out = 1 + max(t for t, _ in pieces)
    outs = [jnp.where(low_half, pieces[(t, 0)], pieces[(t, 1)]) for t in range(n_out)]
    return (outs[0] if n_out == 1 else jnp.concatenate(outs, axis=1)).astype(BF16)


def _attn_body(*refs, n_seg, plan, use_sink):
    q_ref = refs[0]
    kv_refs = refs[1:1 + 2 * n_seg]
    sink_ref = refs[1 + 2 * n_seg] if use_sink else None
    o_ref = refs[-1]

    def segs_of(k0, v0):
        return [(kv_refs[2 * s][0, :, k0:k0 + LANES], kv_refs[2 * s + 1][0, v0:v0 + LANES, :], None)
                for s in range(n_seg)]

    o_ref[0] = _attn_units(q_ref[0], segs_of, sink_ref, plan)


def _attn(q, segs, sink, *, hq, hkv, dq, dv, tq, name):
    b, n, _ = q.shape
    tq = min(tq, n)
    in_specs = [pl.BlockSpec((1, tq, hq * dq), lambda bb, i: (bb, i, 0))]
    args = [q]
    for k, vt in segs:
        in_specs.append(pl.BlockSpec((1,) + k.shape[1:], lambda bb, i: (bb, 0, 0)))
        in_specs.append(pl.BlockSpec((1,) + vt.shape[1:], lambda bb, i: (bb, 0, 0)))
        args += [k, vt]
    if sink is not None:
        in_specs.append(pl.BlockSpec(memory_space=pltpu.SMEM))
        args.append(sink)
    body = functools.partial(_attn_body, n_seg=len(segs), plan=_attn_plan(hq, hkv, dq), use_sink=sink is not None)
    return pl.pallas_call(
        body,
        out_shape=jax.ShapeDtypeStruct((b, n, hq * dv), BF16),
        grid=(b, n // tq),
        in_specs=in_specs,
        out_specs=pl.BlockSpec((1, tq, hq * dv), lambda bb, i: (bb, i, 0)),
        compiler_params=_params("arbitrary", "arbitrary"),
        name=name,
    )(*args)


def _attn_a_body(*refs, seq, tq, nkb):
    q_ref = refs[0]
    k_refs, v_refs = refs[1:1 + nkb], refs[1 + nkb:1 + 2 * nkb]
    ck_ref, cvt_ref, sink_ref, o_ref = refs[1 + 2 * nkb:]
    start = pl.program_id(1) * tq
    span = nkb * WINDOW
    k_win = jnp.concatenate([r[0] for r in k_refs], axis=0)
    vt_win = jnp.concatenate([r[0] for r in v_refs], axis=1)
    kpos = start - WINDOW + lax.broadcasted_iota(jnp.int32, (span, tq), 0)
    qpos = start + lax.broadcasted_iota(jnp.int32, (span, tq), 1)
    valid = (kpos >= 0) & (kpos < seq) & (jnp.abs(qpos - kpos) <= WINDOW)
    valid = jnp.concatenate([valid] * (A_HEADS // A_KV_HEADS), axis=1)

    def segs_of(k0, v0):
        return [(k_win, vt_win, valid), (ck_ref[0], cvt_ref[0], None)]

    o_ref[0] = _attn_units(q_ref[0], segs_of, sink_ref, _attn_plan(A_HEADS, A_KV_HEADS, HEAD_DIM))


def _attn_a(q, k, vt, ck, cvt, sink, tq):
    b, n, w = q.shape
    assert WINDOW == LANES and tq % WINDOW == 0 and n % tq == 0
    per = tq // WINDOW
    nkb = per + 2
    last = n // WINDOW - 1
    blk = lambda j: (lambda i: jnp.clip(i * per + j - 1, 0, last))
    k_blk = lambda f: pl.BlockSpec((1, WINDOW, LANES), lambda bb, i: (bb, f(i), 0))
    v_blk = lambda f: pl.BlockSpec((1, LANES, WINDOW), lambda bb, i: (bb, 0, f(i)))
    whole = lambda a: pl.BlockSpec((1,) + a.shape[1:], lambda bb, i: (bb, 0, 0))
    return pl.pallas_call(
        functools.partial(_attn_a_body, seq=n, tq=tq, nkb=nkb),
        out_shape=jax.ShapeDtypeStruct((b, n, w), BF16),
        grid=(b, n // tq),
        in_specs=[pl.BlockSpec((1, tq, w), lambda bb, i: (bb, i, 0))]
                 + [k_blk(blk(j)) for j in range(nkb)] + [v_blk(blk(j)) for j in range(nkb)]
                 + [whole(ck), whole(cvt), pl.BlockSpec(memory_space=pltpu.SMEM)],
        out_specs=pl.BlockSpec((1, tq, w), lambda bb, i: (bb, i, 0)),
        compiler_params=_params("arbitrary", "arbitrary"),
        name="attn_a_window",
    )(q, *([k] * nkb), *([vt] * nkb), ck, cvt, sink)


def _sigmoid(z):
    return 1.0 / (1.0 + jnp.exp(-z))


def _layer_norm(z, g, b):
    mu = jnp.mean(z, axis=-1, keepdims=True)
    zc = z - mu
    var = jnp.mean(zc * zc, axis=-1, keepdims=True)
    return zc * lax.rsqrt(var + LN_EPS) * g + b


def _merge_body(oa_ref, ob_ref, oc_ref, x_ref, sc1_ref, sh1_ref, g1_ref, sc2_ref, sh2_ref, lng_ref, lnb_ref,
                wg_ref, wa_ref, wb_ref, wc_ref, wo_ref, wr_ref, x1_ref, h2_ref, aff_ref, *, alpha):
    d = x_ref.shape[-1]
    x = x_ref[0]
    h1 = (x * (1.0 + sc1_ref[0]) + sh1_ref[0]).astype(BF16)
    merged = 0.0
    for j, (o_ref, w_ref) in enumerate(((oa_ref, wa_ref), (ob_ref, wb_ref), (oc_ref, wc_ref))):
        gate = _sigmoid(_dot(h1, wg_ref[:, j * d:(j + 1) * d]))
        merged = merged + gate * _dot(o_ref[0], w_ref[...])
    y = _dot(merged.astype(BF16), wo_ref[...])
    x1 = _layer_norm(alpha * x + g1_ref[0] * y, lng_ref[...], lnb_ref[...])
    x1_ref[0] = x1
    h2 = (x1 * (1.0 + sc2_ref[0]) + sh2_ref[0]).astype(BF16)
    h2_ref[0] = _pack_pairs(h2)
    logits = _dot(h2, wr_ref[...])
    lane = lax.broadcasted_iota(jnp.int32, logits.shape, 1)
    logits = jnp.where(lane < N_EXPERTS, logits, NEG_BIG)
    e = jnp.exp(logits - logits.max(axis=-1, keepdims=True))
    aff_ref[0] = e / e.sum(axis=-1, keepdims=True)


def _merge(oa, ob, oc, x, sc1, sh1, g1, sc2, sh2, lng, lnb, wg, wa, wb, wc, wo, wr, alpha, tm):
    b, n, d = x.shape
    tm = min(tm, n)
    tok = lambda w: pl.BlockSpec((1, tm, w), lambda bb, i: (bb, i, 0))
    mod = pl.BlockSpec((1, 1, d), lambda bb, i: (bb, 0, 0))
    full = lambda a: pl.BlockSpec(a.shape, lambda bb, i: (0,) * a.ndim, pipeline_mode=pl.Buffered(1))
    return pl.pallas_call(
        functools.partial(_merge_body, alpha=alpha),
        out_shape=[jax.ShapeDtypeStruct((b, n, d), F32), jax.ShapeDtypeStruct((b, n, d // 2), jnp.uint32),
                   jax.ShapeDtypeStruct((b, n, LANES), F32)],
        grid=(b, n // tm),
        in_specs=[tok(oa.shape[2]), tok(ob.shape[2]), tok(oc.shape[2]), tok(d), mod, mod, mod, mod, mod,
                  full(lng), full(lnb), full(wg), full(wa), full(wb), full(wc), full(wo), full(wr)],
        out_specs=[tok(d), tok(d // 2), tok(LANES)],
        compiler_params=_params("arbitrary", "arbitrary"),
        name="merge",
    )(oa, ob, oc, x, sc1, sh1, g1, sc2, sh2, lng, lnb, wg, wa, wb, wc, wo, wr)


def _route_body(aff_ref, tri_ref, slot_ref, *idx_refs, cap):
    bits = lax.bitcast_convert_type(aff_ref[...], jnp.int32)
    count = lambda m: jnp.sum(jnp.where(m, 1.0, 0.0), axis=1, keepdims=True)
    thr = jnp.zeros((bits.shape[0], 1), jnp.int32)
    for bit in range(30, -1, -1):
        cand = thr | (1 << bit)
        thr = jnp.where(count(bits >= cand) >= cap, cand, thr)
    above = bits > thr
    tie = bits == thr
    need = cap - count(above)
    tri = tri_ref[...]
    tie_rank = _dot(jnp.where(tie, 1.0, 0.0).astype(BF16), tri)
    sel = above | (tie & (tie_rank < need))
    pos = _dot(jnp.where(sel, 1.0, 0.0).astype(BF16), tri)
    slot_ref[...] = jnp.where(sel, pos.astype(jnp.int32), -1)
    if not idx_refs:
        return
    idx_ref, cum_ref = idx_refs
    rows, n = bits.shape
    cum_ref[...] = pos + jnp.where(sel, 1.0, 0.0)
    slot_id = lax.broadcasted_iota(jnp.int32, (cap, n), 0).astype(F32)
    lane = lax.broadcasted_iota(jnp.int32, (cap, LANES), 1)
    for blk in range(rows // LANES):
        def one_row(r, tile, blk=blk):
            below = cum_ref[pl.ds(blk * LANES + r, 1), :] <= slot_id
            return jnp.where(lane == r, jnp.sum(jnp.where(below, 1.0, 0.0), axis=1, keepdims=True), tile)

        tile = lax.fori_loop(0, LANES, one_row, jnp.zeros((cap, LANES), F32))
        idx_ref[blk * LANES:(blk + 1) * LANES, :] = tile.T.astype(jnp.int32)


def _route(aff_t, cap, with_idx):
    rows, n = aff_t.shape
    tri = jnp.triu(jnp.ones((n, n), BF16), k=1)
    out_shape = [jax.ShapeDtypeStruct((rows, n), jnp.int32)]
    if with_idx:
        out_shape.append(jax.ShapeDtypeStruct((rows, cap), jnp.int32))
    return pl.pallas_call(
        functools.partial(_route_body, cap=cap),
        out_shape=out_shape,
        scratch_shapes=[pltpu.VMEM((rows, n), F32)] if with_idx else [],
        compiler_params=pltpu.CompilerParams(vmem_limit_bytes=VMEM_LIMIT),
        name="route",
    )(aff_t, tri)


def _pack_pairs(h):
    half = h.shape[1] // 2
    lo = lax.bitcast_convert_type(h[:, :half].astype(F32), jnp.uint32) >> 16
    hi = lax.bitcast_convert_type(h[:, half:].astype(F32), jnp.uint32) & jnp.uint32(0xFFFF0000)
    return hi | lo


def _unpack_pairs(p):
    lo = lax.bitcast_convert_type(p << 16, F32).astype(BF16)
    hi = lax.bitcast_convert_type(p & jnp.uint32(0xFFFF0000), F32).astype(BF16)
    return jnp.concatenate([lo, hi], axis=1)


def _experts_body(idx_ref, h_ref, slot_row_ref, aff_row_ref, slot_tok_ref, wg_ref, wu_ref, wd_ref, o_ref, *scratch,
                  cap, row_gather):
    e = pl.program_id(1)
    last = pl.num_programs(1) - 1
    group, n, _ = h_ref.shape
    picks = [slot_row_ref[s, 0] == lax.broadcasted_iota(jnp.int32, (cap, n), 0) for s in range(group)]

    def copy_rows(expert, dst_ref):
        base = (pl.program_id(0) * (last + 1) + expert) * cap
        for j in range(cap):
            dst_ref[j:j + 1, :] = h_ref[0, pl.ds(idx_ref[base + j], 1), :]

    def add_rows(expert, src_ref):
        base = (pl.program_id(0) * (last + 1) + expert) * cap
        for j in range(cap):
            o_ref[0, pl.ds(idx_ref[base + j], 1), :] += src_ref[j:j + 1, :]

    def ffn(xs):
        aff = [jnp.sum(jnp.where(pick, aff_row_ref[s, 0], 0.0), axis=1, keepdims=True) for s, pick in enumerate(picks)]
        aff = aff[0] if group == 1 else jnp.concatenate(aff, axis=0)
        hid = _dot(xs, wg_ref[0, 0])
        hid = hid * _sigmoid(hid) * _dot(xs, wu_ref[0, 0])
        return _dot(hid.astype(BF16), wd_ref[0, 0]) * aff

    @pl.when(e == 0)
    def _():
        o_ref[...] = jnp.zeros_like(o_ref)
        if row_gather:
            copy_rows(0, scratch[0])
            scratch[3][...] = jnp.zeros_like(scratch[3])

    if not row_gather:
        xs = [_dot(jnp.where(pick, 1.0, 0.0).astype(BF16), _unpack_pairs(h_ref[s])).astype(BF16)
              for s, pick in enumerate(picks)]
        ys = ffn(xs[0] if group == 1 else jnp.concatenate(xs, axis=0)).astype(BF16)
        is_e = lax.broadcasted_iota(jnp.int32, (n, LANES), 1) == e
        for s in range(group):
            slot_col = jnp.sum(jnp.where(is_e, slot_tok_ref[s], 0.0), axis=1, keepdims=True)
            put = jnp.where(slot_col == lax.broadcasted_iota(jnp.int32, (n, cap), 1).astype(F32), 1.0, 0.0)
            o_ref[s] += _dot(put.astype(BF16), ys[s * cap:(s + 1) * cap])
        return

    def step(rows_ref, next_rows_ref, ys_ref, prev_ys_ref):
        copy_rows(jnp.minimum(e + 1, last), next_rows_ref)
        add_rows(jnp.maximum(e - 1, 0), prev_ys_ref)
        ys_ref[...] = ffn(_unpack_pairs(rows_ref[...]))

        @pl.when(e == last)
        def _():
            add_rows(last, ys_ref)

    @pl.when(e % 2 == 0)
    def _():
        step(scratch[0], scratch[1], scratch[2], scratch[3])

    @pl.when(e % 2 == 1)
    def _():
        step(scratch[1], scratch[0], scratch[3], scratch[2])


def _experts(h2p, slot_t, idx, aff_t, wg, wu, wd, layer, cap, group):
    b, n, half = h2p.shape
    d = 2 * half
    _, ne, _, f = wg.shape
    row_gather = idx is not None
    slot_rows = slot_t.reshape(b, ne, 1, n)
    aff_rows = aff_t.reshape(b, ne, 1, n)
    slot_tok = jnp.transpose(slot_t.reshape(b, ne, n), (0, 2, 1)).astype(F32)
    slot_tok = jnp.pad(slot_tok, ((0, 0), (0, 0), (0, LANES - ne)))
    row = pl.BlockSpec((group, 1, 1, n), lambda bb, e, idx: (bb, e, 0, 0))
    return pl.pallas_call(
        functools.partial(_experts_body, cap=cap, row_gather=row_gather),
        out_shape=jax.ShapeDtypeStruct((b, n, d), F32),
        grid_spec=pltpu.PrefetchScalarGridSpec(
            num_scalar_prefetch=1,
            grid=(b // group, ne),
            in_specs=[pl.BlockSpec((group, n, half), lambda bb, e, idx: (bb, 0, 0)), row, row,
                      pl.BlockSpec((group, n, LANES), lambda bb, e, idx: (bb, 0, 0)),
                      pl.BlockSpec((1, 1, d, f), lambda bb, e, idx: (layer, e, 0, 0)),
                      pl.BlockSpec((1, 1, d, f), lambda bb, e, idx: (layer, e, 0, 0)),
                      pl.BlockSpec((1, 1, f, d), lambda bb, e, idx: (layer, e, 0, 0))],
            out_specs=pl.BlockSpec((group, n, d), lambda bb, e, idx: (bb, 0, 0)),
            scratch_shapes=([pltpu.VMEM((cap, half), jnp.uint32)] * 2 + [pltpu.VMEM((cap, d), F32)] * 2
                            if row_gather else [])),
        compiler_params=_params("arbitrary", "arbitrary"),
        name="experts",
    )(idx.reshape(-1) if row_gather else jnp.zeros((1,), jnp.int32), h2p, slot_rows, aff_rows, slot_tok, wg, wu, wd)


def _moe(h2p, aff, wg, wu, wd, layer):
    b, n, _ = h2p.shape
    cap = CAPACITY_FACTOR * n // N_EXPERTS
    group = min(b, max(1, MXU_ROWS // cap))
    while b % group:
        group -= 1
    rows = b * N_EXPERTS
    row_gather = group == 1 and rows % LANES == 0 and cap % LANES == 0
    aff_t = jnp.transpose(aff[:, :, :N_EXPERTS], (0, 2, 1)).reshape(rows, n)
    outs = _route(aff_t, cap, row_gather)
    return _experts(h2p, outs[0], outs[1] if row_gather else None, aff_t, wg, wu, wd, layer, cap, group)


def _ln2_body(x_ref, y_ref, g2_ref, lng_ref, lnb_ref, o_ref, *, alpha):
    o_ref[0] = _layer_norm(alpha * x_ref[0] + g2_ref[0] * y_ref[0], lng_ref[...], lnb_ref[...])


def _ln2(x1, y, g2, lng, lnb, alpha, tm):
    b, n, d = x1.shape
    tm = min(tm, n)
    tok = pl.BlockSpec((1, tm, d), lambda bb, i: (bb, i, 0))
    vec = pl.BlockSpec((1, d), lambda bb, i: (0, 0))
    return pl.pallas_call(
        functools.partial(_ln2_body, alpha=alpha),
        out_shape=jax.ShapeDtypeStruct((b, n, d), F32),
        grid=(b, n // tm),
        in_specs=[tok, tok, pl.BlockSpec((1, 1, d), lambda bb, i: (bb, 0, 0)), vec, vec],
        out_specs=tok,
        compiler_params=_params("arbitrary", "arbitrary"),
        name="ln2",
    )(x1, y, g2, lng, lnb)


def _rope_tables(n, enabled):
    lane = np.arange(LANES)
    if not enabled:
        one, zero = jnp.ones((n, LANES), F32), jnp.zeros((n, LANES), F32)
        return one, zero, one, zero
    t = jnp.arange(n)
    pos = jnp.stack([t // GRID_W, t % GRID_W], axis=1).astype(F32)

    def build(jj, d, active):
        freqs = ROPE_THETA ** (-jnp.arange(d, dtype=F32) / d)
        ang = pos[:, (jj // (2 * d)) % 2] * freqs[jj % d][None, :]
        sign = np.where((jj // d) % 2 == 0, -1.0, 1.0).astype(np.float32)
        cos = jnp.where(active[None, :], jnp.cos(ang), 1.0)
        sin = jnp.where(active[None, :], jnp.sin(ang) * sign[None, :], 0.0)
        return cos, sin

    cos_a, sin_a = build(lane % HEAD_DIM, HEAD_DIM // 4, np.ones(LANES, bool))
    in_rope = (lane >= B_NOPE) & (lane < B_NOPE + B_ROPE)
    cos_b, sin_b = build(np.where(in_rope, lane - B_NOPE, 0), B_ROPE // 4, in_rope)
    return cos_a, sin_a, cos_b, sin_b


def _pair_heads(w, axis, heads, groups):
    shape = w.shape
    split = shape[:axis] + (groups, heads // groups, HEAD_DIM) + shape[axis + 1:]
    return jnp.swapaxes(w.reshape(split), axis, axis + 1).reshape(shape)


def _layer_weights(w_in, b_q_norm, b_kv_norm, b_w_uq, b_w_ukv, c_q_norm, c_k_norm):
    d = w_in.shape[0]
    sizes = (A_HEADS * HEAD_DIM, A_KV_HEADS * HEAD_DIM, A_KV_HEADS * HEAD_DIM, B_Q_LORA, B_KV_LORA, B_ROPE,
             C_HEADS * HEAD_DIM, C_KV_HEADS * HEAD_DIM, C_KV_HEADS * HEAD_DIM)
    offs = np.cumsum((0,) + sizes)
    a_q, a_k, a_v, b_cq, b_ckv, b_kr, c_q, c_k, c_v = [w_in[:, offs[i]:offs[i + 1]] for i in range(9)]
    gate = w_in[:, offs[9]:]
    zeros = lambda w: jnp.zeros((d, w), w_in.dtype)
    w1 = jnp.concatenate([_pair_heads(a_q, 1, A_HEADS, A_KV_HEADS), a_k], axis=1)
    w2 = jnp.concatenate([_pair_heads(c_q, 1, C_HEADS, C_KV_HEADS), c_k], axis=1)
    w3 = jnp.concatenate([a_v, c_v], axis=1)
    w4 = jnp.concatenate([b_cq, b_ckv, zeros(B_NOPE), b_kr, zeros(B_HEAD_PAD - B_NOPE - B_ROPE)], axis=1)
    uq = b_w_uq.reshape(B_Q_LORA, B_HEADS, B_NOPE + B_ROPE)
    uq = jnp.pad(uq, ((0, 0), (0, 0), (0, B_HEAD_PAD - B_NOPE - B_ROPE))).reshape(B_Q_LORA, B_HEADS * B_HEAD_PAD)
    ukv = b_w_ukv.reshape(B_KV_LORA, B_HEADS, B_NOPE + B_V)
    uk = jnp.pad(ukv[:, :, :B_NOPE], ((0, 0), (0, 0), (0, B_HEAD_PAD - B_NOPE))).reshape(B_KV_LORA, B_HEADS * B_HEAD_PAD)
    uv = ukv[:, :, B_NOPE:].reshape(B_KV_LORA, B_HEADS * B_V)
    wuk = jnp.concatenate([uk, uv], axis=1)
    width = w2.shape[1]
    blk = np.arange(width) // HEAD_DIM
    bd = jnp.asarray((blk[:, None] == blk[None, :]).astype(np.float32) / HEAD_DIM, BF16)
    gc = jnp.concatenate([jnp.tile(c_q_norm, C_HEADS), jnp.tile(c_k_norm, C_KV_HEADS)])[None, :]
    cast = lambda a: a.astype(BF16)
    return (cast(jnp.concatenate([w1, w2, w3, w4], axis=1)), cast(uq), cast(wuk), bd,
            b_q_norm[None, :], b_kv_norm[None, :], gc), cast(gate)


def kernel(x, c, ctx, c_ctx, w_ada, b_ada, w_in, a_sink, b_q_norm, b_kv_norm, b_w_uq, b_w_ukv, c_q_norm, c_k_norm,
           w_branch, w_out, ln1_g, ln1_b, w_router, w_gate, w_up, w_down, ln2_g, ln2_b):
    bsz, n, d = x.shape
    depth = w_ada.shape[0]
    alpha = (2 * depth) ** 0.25
    aq, bo = A_HEADS * HEAD_DIM, B_HEADS * B_V
    tm = 512

    rows = -(-(bsz + 1) // 8) * 8
    cc = jnp.concatenate([c, c_ctx[None, :], jnp.zeros((rows - bsz - 1, d), c.dtype)], axis=0)
    mod = _ada(cc, w_ada, b_ada)
    tabs_lat = _rope_tables(n, True)
    tabs_ctx = _rope_tables(ctx.shape[1], False)

    b_kw = dict(hq=B_HEADS, hkv=B_HEADS, dq=B_HEAD_PAD, dv=B_V, tq=1024)
    c_kw = dict(hq=C_HEADS, hkv=C_KV_HEADS, dq=HEAD_DIM, dv=HEAD_DIM, tq=512)
    a_kw = dict(hq=A_HEADS, hkv=A_KV_HEADS, dq=HEAD_DIM, dv=HEAD_DIM, tq=256)

    ffn_bf16 = (w_gate.astype(BF16), w_up.astype(BF16), w_down.astype(BF16))
    cx = ctx
    for l in range(depth):
        last = l == depth - 1
        lat_mod = [m[:, None, :] for m in jnp.split(mod[l, :bsz], 6, axis=-1)]
        ctx_mod = [jnp.broadcast_to(m[None, None, :], (bsz, 1, d)) for m in jnp.split(mod[l, bsz], 6, axis=-1)]
        sh1, sc1, g1, sh2, sc2, g2 = lat_mod
        sh1c, sc1c, g1c, sh2c, sc2c, g2c = ctx_mod
        wts, w_gl = _layer_weights(w_in[l], b_q_norm[l], b_kv_norm[l], b_w_uq[l], b_w_ukv[l], c_q_norm[l], c_k_norm[l])
        wbr = w_branch[l].astype(BF16)
        merge_w = (w_gl, _pair_heads(wbr[:aq], 0, A_HEADS, A_KV_HEADS), wbr[aq:aq + bo],
                   _pair_heads(wbr[aq + bo:], 0, C_HEADS, C_KV_HEADS), w_out[l].astype(BF16),
                   jnp.pad(w_router[l], ((0, 0), (0, LANES - N_EXPERTS))).astype(BF16))
        ffn_w = (*ffn_bf16, l)
        ln1 = (ln1_g[l][None, :], ln1_b[l][None, :])
        ln2 = (ln2_g[l][None, :], ln2_b[l][None, :])

        qa, ka, va, qb, kb, vb, qc, kc, vc = _inproj(x, sc1, sh1, wts, tabs_lat, 2 * tm)
        cqa, cka, cva, cqb, ckb, cvb, cqc, ckc, cvc = _inproj(cx, sc1c, sh1c, wts, tabs_ctx, tm)

        o_a = _attn_a(qa, ka, va, cka, cva, a_sink[l], min(n, 256))
        o_b = _attn(qb, [(ckb, cvb), (kb, vb)], None, name="attn_b", **b_kw)
        o_c = _attn(qc, [(ckc, cvc), (kc, vc)], None, name="attn_c", **c_kw)
        x1, h2, aff = _merge(o_a, o_b, o_c, x, sc1, sh1, g1, sc2, sh2, *ln1, *merge_w, alpha, tm)
        if not last:
            oc_a = _attn(cqa, [(cka, cva)], a_sink[l], name="attn_a_ctx", **a_kw)
            oc_b = _attn(cqb, [(ckb, cvb)], None, name="attn_b_ctx", **b_kw)
            oc_c = _attn(cqc, [(ckc, cvc)], None, name="attn_c_ctx", **c_kw)
            cx1, hc2, affc = _merge(oc_a, oc_b, oc_c, cx, sc1c, sh1c, g1c, sc2c, sh2c, *ln1, *merge_w, alpha, tm)

        x = _ln2(x1, _moe(h2, aff, *ffn_w), g2, *ln2, alpha, tm)
        if not last:
            cx = _ln2(cx1, _moe(hc2, affc, *ffn_w), g2c, *ln2, alpha, tm)
    return x
```

```python
import functools

import jax
import jax.numpy as jnp
import numpy as np
from jax import lax
from jax.experimental import pallas as pl
from jax.experimental.pallas import tpu as pltpu

F32 = jnp.float32
BF16 = jnp.bfloat16

GRID_W = 64
HEAD_DIM = 64
A_HEADS, A_KV_HEADS = 6, 2
WINDOW = 128
B_HEADS, B_NOPE, B_ROPE, B_V = 4, 64, 32, 64
B_Q_LORA, B_KV_LORA = 256, 128
C_HEADS, C_KV_HEADS = 6, 2
N_EXPERTS = 16
CAPACITY_FACTOR = 2
ROPE_THETA = 10000.0
LN_EPS = 1e-5
RMS_EPS = 1e-6
LOG2E = float(np.log2(np.e))
A_SCALE = HEAD_DIM ** -0.5 * LOG2E
B_SCALE = (B_NOPE + B_ROPE) ** -0.5 * LOG2E
C_SCALE = HEAD_DIM ** -0.5 * LOG2E

LANES = 128
BF16_SUBLANES = 16
MXU_ROWS = 256
B_HEAD_PAD = 128
NEG_BIG = -1e30
VMEM_LIMIT = 56 * 1024 * 1024


def _params(*sem):
    return pltpu.CompilerParams(dimension_semantics=sem, vmem_limit_bytes=VMEM_LIMIT)


def _dot(a, b):
    return jnp.dot(a, b, preferred_element_type=F32)


def _dot_nt(a, b):
    return lax.dot_general(a, b, (((1,), (1,)), ((), ())), preferred_element_type=F32)


def _ada_body(c_ref, w_ref, b_ref, o_ref):
    c = c_ref[...]
    s = (c * (1.0 / (1.0 + jnp.exp(-c)))).astype(BF16)
    o_ref[0] = _dot(s, w_ref[0].astype(BF16)) + b_ref[0]


def _ada(cc, w_ada, b_ada):
    depth, d, six_d = w_ada.shape
    rows = cc.shape[0]
    tn = six_d // 4
    return pl.pallas_call(
        _ada_body,
        out_shape=jax.ShapeDtypeStruct((depth, rows, six_d), F32),
        grid=(depth, six_d // tn),
        in_specs=[
            pl.BlockSpec((rows, d), lambda l, j: (0, 0)),
            pl.BlockSpec((1, d, tn), lambda l, j: (l, 0, j)),
            pl.BlockSpec((1, 1, tn), lambda l, j: (l, 0, j)),
        ],
        out_specs=pl.BlockSpec((1, rows, tn), lambda l, j: (l, 0, j)),
        compiler_params=_params("arbitrary", "arbitrary"),
        name="ada",
    )(cc, w_ada, b_ada.reshape(depth, 1, six_d))


def _rope(x, cos, sin, d):
    lane = lax.broadcasted_iota(jnp.int32, (x.shape[0], LANES), 1)
    first = (lane // d) % 2 == 0
    outs = []
    for j in range(x.shape[1] // LANES):
        xs = x[:, j * LANES:(j + 1) * LANES]
        partner = jnp.where(first, pltpu.roll(xs, LANES - d, 1), pltpu.roll(xs, d, 1))
        outs.append(xs * cos + partner * sin)
    return outs[0] if len(outs) == 1 else jnp.concatenate(outs, axis=1)


def _rms(x, gain):
    return x * lax.rsqrt(jnp.mean(x * x, axis=-1, keepdims=True) + RMS_EPS) * gain


def _inproj_body(x_ref, sc_ref, sh_ref, w_ref, wuq_ref, wuk_ref, bd_ref,
                 gq_ref, gkv_ref, gc_ref, cos_a_ref, sin_a_ref, cos_b_ref, sin_b_ref,
                 qa_ref, ka_ref, va_ref, qb_ref, kb_ref, vb_ref, qc_ref, kc_ref, vc_ref):
    h = (x_ref[0] * (1.0 + sc_ref[0]) + sh_ref[0]).astype(BF16)
    cos_a, sin_a = cos_a_ref[...], sin_a_ref[...]
    cos_b, sin_b = cos_b_ref[...], sin_b_ref[...]
    aq = A_HEADS * HEAD_DIM
    cq = C_HEADS * HEAD_DIM
    proj = _dot(h, w_ref[...])
    o_c, o_v, o_b = aq + LANES, aq + cq + 2 * LANES, aq + cq + 4 * LANES

    r = _rope(proj[:, :o_c], cos_a, sin_a, HEAD_DIM // 4)
    qa_ref[0] = (r[:, :aq] * A_SCALE).astype(BF16)
    ka_ref[0] = r[:, aq:].astype(BF16)

    r = proj[:, o_c:o_v]
    sq = r * r
    sq_hi = sq.astype(BF16)
    sq_lo = (sq - sq_hi.astype(F32)).astype(BF16)
    ms = _dot(sq_hi, bd_ref[...]) + _dot(sq_lo, bd_ref[...])
    r = _rope(r * lax.rsqrt(ms + RMS_EPS) * gc_ref[...], cos_a, sin_a, HEAD_DIM // 4)
    qc_ref[0] = (r[:, :cq] * C_SCALE).astype(BF16)
    kc_ref[0] = r[:, cq:].astype(BF16)

    r = proj[:, o_v:o_b].T
    va_ref[0] = r[:LANES].astype(BF16)
    vc_ref[0] = r[LANES:].astype(BF16)

    r = proj[:, o_b:]
    c_q = _rms(r[:, :B_Q_LORA], gq_ref[...]).astype(BF16)
    c_kv = _rms(r[:, B_Q_LORA:B_Q_LORA + B_KV_LORA], gkv_ref[...]).astype(BF16)
    k_rope = _rope(r[:, B_Q_LORA + B_KV_LORA:], cos_b, sin_b, B_ROPE // 4)
    q = _rope(_dot(c_q, wuq_ref[...]), cos_b, sin_b, B_ROPE // 4)
    qb_ref[0] = (q * B_SCALE).astype(BF16)
    kv = _dot(c_kv, wuk_ref[...])
    kw = B_HEADS * B_HEAD_PAD
    kb_ref[0] = (kv[:, :kw] + jnp.concatenate([k_rope] * B_HEADS, axis=1)).astype(BF16)
    vb_ref[0] = kv[:, kw:].T.astype(BF16)


def _inproj(x, sc, sh, wts, tabs, tm):
    b, n, d = x.shape
    tm = min(tm, n)
    cos_a, sin_a, cos_b, sin_b = tabs
    full = lambda a: pl.BlockSpec(a.shape, lambda i, bb: (0,) * a.ndim, pipeline_mode=pl.Buffered(1))
    tok = lambda w: pl.BlockSpec((1, tm, w), lambda i, bb: (bb, i, 0))
    mod = pl.BlockSpec((1, 1, d), lambda i, bb: (bb, 0, 0))
    tab = pl.BlockSpec((tm, LANES), lambda i, bb: (i, 0))
    widths = (A_HEADS * HEAD_DIM, LANES, -LANES, B_HEADS * B_HEAD_PAD, B_HEADS * B_HEAD_PAD, -B_HEADS * B_V,
              C_HEADS * HEAD_DIM, LANES, -LANES)
    tok_t = lambda w: pl.BlockSpec((1, w, tm), lambda i, bb: (bb, 0, i))
    return pl.pallas_call(
        _inproj_body,
        out_shape=[jax.ShapeDtypeStruct((b, n, w) if w > 0 else (b, -w, n), BF16) for w in widths],
        grid=(n // tm, b),
        in_specs=[tok(d), mod, mod] + [full(a) for a in wts] + [tab] * 4,
        out_specs=[tok(w) if w > 0 else tok_t(-w) for w in widths],
        compiler_params=_params("arbitrary", "arbitrary"),
        name="inproj",
    )(x, sc, sh, *wts, cos_a, sin_a, cos_b, sin_b)


def _attend_t(q, segs, sink):
    scores = []
    for k, _, valid in segs:
        st = _dot_nt(k, q)
        scores.append(st if valid is None else jnp.where(valid, st, NEG_BIG))
    m = scores[0].max(axis=0, keepdims=True)
    for st in scores[1:]:
        m = jnp.maximum(m, st.max(axis=0, keepdims=True))
    if sink is not None:
        m = jnp.maximum(m, sink)
    out_t = 0.0
    for st, (_, vt, _) in zip(scores, segs):
        vt1 = jnp.concatenate([vt, jnp.ones((BF16_SUBLANES, vt.shape[1]), BF16)], axis=0)
        out_t = out_t + _dot(vt1, jnp.exp2(st - m).astype(BF16))
    den = out_t[LANES:LANES + 1]
    if sink is not None:
        den = den + jnp.exp2(sink - m)
    return (out_t[:LANES] / den).T


def _attn_plan(hq, hkv, dq):
    if dq == LANES:
        return [([h], None, h * LANES, (h // 2) * LANES, [h], h % 2, [h // 2]) for h in range(hq)]
    rep = hq // hkv
    assert hkv == 2 and dq == LANES // 2
    return [(list(range(rep)), g, 0, 0, [g * rep + r for r in range(rep)], g, list(range(rep))) for g in range(hkv)]


def _attn_units(q, segs_of, sink_ref, plan):
    tq = q.shape[0]
    low_half = lax.broadcasted_iota(jnp.int32, (tq, LANES), 1) < LANES // 2
    pieces = {}
    for tiles, keep, k0, v0, heads, out_half, out_tiles in plan:
        qs = []
        for t in tiles:
            qt = q[:, t * LANES:(t + 1) * LANES]
            if keep is not None:
                qt = jnp.where(low_half == (keep == 0), qt, jnp.zeros_like(qt))
            qs.append(qt)
        qg = qs[0] if len(qs) == 1 else jnp.concatenate(qs, axis=0)
        sink = None
        if sink_ref is not None:
            sink = jnp.concatenate([jnp.full((1, tq), sink_ref[hd] * LOG2E, F32) for hd in heads], axis=1)
        og = _attend_t(qg, segs_of(k0, v0), sink)
        for j, t in enumerate(out_tiles):
            pieces[(t, out_half)] = og[j * tq:(j + 1) * tq]
    n_out = 1 + max(t for t, _ in pieces)
    outs = [jnp.where(low_half, pieces[(t, 0)], pieces[(t, 1)]) for t in range(n_out)]
    return (outs[0] if n_out == 1 else jnp.concatenate(outs, axis=1)).astype(BF16)


def _attn_body(*refs, n_seg, plan, use_sink):
    q_ref = refs[0]
    kv_refs = refs[1:1 + 2 * n_seg]
    sink_ref = refs[1 + 2 * n_seg] if use_sink else None
    o_ref = refs[-1]

    def segs_of(k0, v0):
        return [(kv_refs[2 * s][0, :, k0:k0 + LANES], kv_refs[2 * s + 1][0, v0:v0 + LANES, :], None)
                for s in range(n_seg)]

    o_ref[0] = _attn_units(q_ref[0], segs_of, sink_ref, plan)


def _attn(q, segs, sink, *, hq, hkv, dq, dv, tq, name):
    b, n, _ = q.shape
    tq = min(tq, n)
    in_specs = [pl.BlockSpec((1, tq, hq * dq), lambda bb, i: (bb, i, 0))]
    args = [q]
    for k, vt in segs:
        in_specs.append(pl.BlockSpec((1,) + k.shape[1:], lambda bb, i: (bb, 0, 0)))
        in_specs.append(pl.BlockSpec((1,) + vt.shape[1:], lambda bb, i: (bb, 0, 0)))
        args += [k, vt]
    if sink is not None:
        in_specs.append(pl.BlockSpec(memory_space=pltpu.SMEM))
        args.append(sink)
    body = functools.partial(_attn_body, n_seg=len(segs), plan=_attn_plan(hq, hkv, dq), use_sink=sink is not None)
    return pl.pallas_call(
        body,
        out_shape=jax.ShapeDtypeStruct((b, n, hq * dv), BF16),
        grid=(b, n // tq),
        in_specs=in_specs,
        out_specs=pl.BlockSpec((1, tq, hq * dv), lambda bb, i: (bb, i, 0)),
        compiler_params=_params("arbitrary", "arbitrary"),
        name=name,
    )(*args)


def _attn_a_body(*refs, seq, tq, nkb):
    q_ref = refs[0]
    k_refs, v_refs = refs[1:1 + nkb], refs[1 + nkb:1 + 2 * nkb]
    ck_ref, cvt_ref, sink_ref, o_ref = refs[1 + 2 * nkb:]
    start = pl.program_id(1) * tq
    span = nkb * WINDOW
    k_win = jnp.concatenate([r[0] for r in k_refs], axis=0)
    vt_win = jnp.concatenate([r[0] for r in v_refs], axis=1)
    kpos = start - WINDOW + lax.broadcasted_iota(jnp.int32, (span, tq), 0)
    qpos = start + lax.broadcasted_iota(jnp.int32, (span, tq), 1)
    valid = (kpos >= 0) & (kpos < seq) & (jnp.abs(qpos - kpos) <= WINDOW)
    valid = jnp.concatenate([valid] * (A_HEADS // A_KV_HEADS), axis=1)

    def segs_of(k0, v0):
        return [(k_win, vt_win, valid), (ck_ref[0], cvt_ref[0], None)]

    o_ref[0] = _attn_units(q_ref[0], segs_of, sink_ref, _attn_plan(A_HEADS, A_KV_HEADS, HEAD_DIM))


def _attn_a(q, k, vt, ck, cvt, sink, tq):
    b, n, w = q.shape
    assert WINDOW == LANES and tq % WINDOW == 0 and n % tq == 0
    per = tq // WINDOW
    nkb = per + 2
    last = n // WINDOW - 1
    blk = lambda j: (lambda i: jnp.clip(i * per + j - 1, 0, last))
    k_blk = lambda f: pl.BlockSpec((1, WINDOW, LANES), lambda bb, i: (bb, f(i), 0))
    v_blk = lambda f: pl.BlockSpec((1, LANES, WINDOW), lambda bb, i: (bb, 0, f(i)))
    whole = lambda a: pl.BlockSpec((1,) + a.shape[1:], lambda bb, i: (bb, 0, 0))
    return pl.pallas_call(
        functools.partial(_attn_a_body, seq=n, tq=tq, nkb=nkb),
        out_shape=jax.ShapeDtypeStruct((b, n, w), BF16),
        grid=(b, n // tq),
        in_specs=[pl.BlockSpec((1, tq, w), lambda bb, i: (bb, i, 0))]
                 + [k_blk(blk(j)) for j in range(nkb)] + [v_blk(blk(j)) for j in range(nkb)]
                 + [whole(ck), whole(cvt), pl.BlockSpec(memory_space=pltpu.SMEM)],
        out_specs=pl.BlockSpec((1, tq, w), lambda bb, i: (bb, i, 0)),
        compiler_params=_params("arbitrary", "arbitrary"),
        name="attn_a_window",
    )(q, *([k] * nkb), *([vt] * nkb), ck, cvt, sink)


def _sigmoid(z):
    return 1.0 / (1.0 + jnp.exp(-z))


def _layer_norm(z, g, b):
    mu = jnp.mean(z, axis=-1, keepdims=True)
    zc = z - mu
    var = jnp.mean(zc * zc, axis=-1, keepdims=True)
    return zc * lax.rsqrt(var + LN_EPS) * g + b


def _merge_body(oa_ref, ob_ref, oc_ref, x_ref, sc1_ref, sh1_ref, g1_ref, sc2_ref, sh2_ref, lng_ref, lnb_ref,
                wg_ref, wa_ref, wb_ref, wc_ref, wo_ref, wr_ref, x1_ref, h2_ref, aff_ref, *, alpha):
    d = x_ref.shape[-1]
    x = x_ref[0]
    h1 = (x * (1.0 + sc1_ref[0]) + sh1_ref[0]).astype(BF16)
    merged = 0.0
    for j, (o_ref, w_ref) in enumerate(((oa_ref, wa_ref), (ob_ref, wb_ref), (oc_ref, wc_ref))):
        gate = _sigmoid(_dot(h1, wg_ref[:, j * d:(j + 1) * d]))
        merged = merged + gate * _dot(o_ref[0], w_ref[...])
    y = _dot(merged.astype(BF16), wo_ref[...])
    x1 = _layer_norm(alpha * x + g1_ref[0] * y, lng_ref[...], lnb_ref[...])
    x1_ref[0] = x1
    h2 = (x1 * (1.0 + sc2_ref[0]) + sh2_ref[0]).astype(BF16)
    h2_ref[0] = _pack_pairs(h2)
    logits = _dot(h2, wr_ref[...])
    lane = lax.broadcasted_iota(jnp.int32, logits.shape, 1)
    logits = jnp.where(lane < N_EXPERTS, logits, NEG_BIG)
    e = jnp.exp(logits - logits.max(axis=-1, keepdims=True))
    aff_ref[0] = e / e.sum(axis=-1, keepdims=True)


def _merge(oa, ob, oc, x, sc1, sh1, g1, sc2, sh2, lng, lnb, wg, wa, wb, wc, wo, wr, alpha, tm):
    b, n, d = x.shape
    tm = min(tm, n)
    tok = lambda w: pl.BlockSpec((1, tm, w), lambda bb, i: (bb, i, 0))
    mod = pl.BlockSpec((1, 1, d), lambda bb, i: (bb, 0, 0))
    full = lambda a: pl.BlockSpec(a.shape, lambda bb, i: (0,) * a.ndim, pipeline_mode=pl.Buffered(1))
    return pl.pallas_call(
        functools.partial(_merge_body, alpha=alpha),
        out_shape=[jax.ShapeDtypeStruct((b, n, d), F32), jax.ShapeDtypeStruct((b, n, d // 2), jnp.uint32),
                   jax.ShapeDtypeStruct((b, n, LANES), F32)],
        grid=(b, n // tm),
        in_specs=[tok(oa.shape[2]), tok(ob.shape[2]), tok(oc.shape[2]), tok(d), mod, mod, mod, mod, mod,
                  full(lng), full(lnb), full(wg), full(wa), full(wb), full(wc), full(wo), full(wr)],
        out_specs=[tok(d), tok(d // 2), tok(LANES)],
        compiler_params=_params("arbitrary", "arbitrary"),
        name="merge",
    )(oa, ob, oc, x, sc1, sh1, g1, sc2, sh2, lng, lnb, wg, wa, wb, wc, wo, wr)


def _route_body(aff_ref, tri_ref, slot_ref, *idx_refs, cap):
    bits = lax.bitcast_convert_type(aff_ref[...], jnp.int32)
    count = lambda m: jnp.sum(jnp.where(m, 1.0, 0.0), axis=1, keepdims=True)
    thr = jnp.zeros((bits.shape[0], 1), jnp.int32)
    for bit in range(30, -1, -1):
        cand = thr | (1 << bit)
        thr = jnp.where(count(bits >= cand) >= cap, cand, thr)
    above = bits > thr
    tie = bits == thr
    need = cap - count(above)
    tri = tri_ref[...]
    tie_rank = _dot(jnp.where(tie, 1.0, 0.0).astype(BF16), tri)
    sel = above | (tie & (tie_rank < need))
    pos = _dot(jnp.where(sel, 1.0, 0.0).astype(BF16), tri)
    slot_ref[...] = jnp.where(sel, pos.astype(jnp.int32), -1)
    if not idx_refs:
        return
    idx_ref, cum_ref = idx_refs
    rows, n = bits.shape
    cum_ref[...] = pos + jnp.where(sel, 1.0, 0.0)
    slot_id = lax.broadcasted_iota(jnp.int32, (cap, n), 0).astype(F32)
    lane = lax.broadcasted_iota(jnp.int32, (cap, LANES), 1)
    for blk in range(rows // LANES):
        def one_row(r, tile, blk=blk):
            below = cum_ref[pl.ds(blk * LANES + r, 1), :] <= slot_id
            return jnp.where(lane == r, jnp.sum(jnp.where(below, 1.0, 0.0), axis=1, keepdims=True), tile)

        tile = lax.fori_loop(0, LANES, one_row, jnp.zeros((cap, LANES), F32))
        idx_ref[blk * LANES:(blk + 1) * LANES, :] = tile.T.astype(jnp.int32)


def _route(aff_t, cap, with_idx):
    rows, n = aff_t.shape
    tri = jnp.triu(jnp.ones((n, n), BF16), k=1)
    out_shape = [jax.ShapeDtypeStruct((rows, n), jnp.int32)]
    if with_idx:
        out_shape.append(jax.ShapeDtypeStruct((rows, cap), jnp.int32))
    return pl.pallas_call(
        functools.partial(_route_body, cap=cap),
        out_shape=out_shape,
        scratch_shapes=[pltpu.VMEM((rows, n), F32)] if with_idx else [],
        compiler_params=pltpu.CompilerParams(vmem_limit_bytes=VMEM_LIMIT),
        name="route",
    )(aff_t, tri)


def _pack_pairs(h):
    half = h.shape[1] // 2
    lo = lax.bitcast_convert_type(h[:, :half].astype(F32), jnp.uint32) >> 16
    hi = lax.bitcast_convert_type(h[:, half:].astype(F32), jnp.uint32) & jnp.uint32(0xFFFF0000)
    return hi | lo


def _unpack_pairs(p):
    lo = lax.bitcast_convert_type(p << 16, F32).astype(BF16)
    hi = lax.bitcast_convert_type(p & jnp.uint32(0xFFFF0000), F32).astype(BF16)
    return jnp.concatenate([lo, hi], axis=1)


def _experts_body(idx_ref, h_ref, slot_row_ref, aff_row_ref, slot_tok_ref, wg_ref, wu_ref, wd_ref, o_ref, *scratch,
                  cap, row_gather):
    e = pl.program_id(1)
    last = pl.num_programs(1) - 1
    group, n, _ = h_ref.shape
    picks = [slot_row_ref[s, 0] == lax.broadcasted_iota(jnp.int32, (cap, n), 0) for s in range(group)]

    def copy_rows(expert, dst_ref):
        base = (pl.program_id(0) * (last + 1) + expert) * cap
        for j in range(cap):
            dst_ref[j:j + 1, :] = h_ref[0, pl.ds(idx_ref[base + j], 1), :]

    def add_rows(expert, src_ref):
        base = (pl.program_id(0) * (last + 1) + expert) * cap
        for j in range(cap):
            t = idx_ref[base + j]
            if j % 2:
                scratch[4][pl.ds(t, 1), :] += src_ref[j:j + 1, :]
            else:
                o_ref[0, pl.ds(t, 1), :] += src_ref[j:j + 1, :]

    def ffn(xs):
        aff = [jnp.sum(jnp.where(pick, aff_row_ref[s, 0], 0.0), axis=1, keepdims=True) for s, pick in enumerate(picks)]
        aff = aff[0] if group == 1 else jnp.concatenate(aff, axis=0)
        hid = _dot(xs, wg_ref[0, 0])
        hid = hid * _sigmoid(hid) * _dot(xs, wu_ref[0, 0])
        return _dot(hid.astype(BF16), wd_ref[0, 0]) * aff

    @pl.when(e == 0)
    def _():
        o_ref[...] = jnp.zeros_like(o_ref)
        if row_gather:
            copy_rows(0, scratch[0])
            scratch[3][...] = jnp.zeros_like(scratch[3])
            scratch[4][...] = jnp.zeros_like(scratch[4])

    if not row_gather:
        xs = [_dot(jnp.where(pick, 1.0, 0.0).astype(BF16), _unpack_pairs(h_ref[s])).astype(BF16)
              for s, pick in enumerate(picks)]
        ys = ffn(xs[0] if group == 1 else jnp.concatenate(xs, axis=0)).astype(BF16)
        is_e = lax.broadcasted_iota(jnp.int32, (n, LANES), 1) == e
        for s in range(group):
            slot_col = jnp.sum(jnp.where(is_e, slot_tok_ref[s], 0.0), axis=1, keepdims=True)
            put = jnp.where(slot_col == lax.broadcasted_iota(jnp.int32, (n, cap), 1).astype(F32), 1.0, 0.0)
            o_ref[s] += _dot(put.astype(BF16), ys[s * cap:(s + 1) * cap])
        return

    def step(rows_ref, next_rows_ref, ys_ref, prev_ys_ref):
        copy_rows(jnp.minimum(e + 1, last), next_rows_ref)
        add_rows(jnp.maximum(e - 1, 0), prev_ys_ref)
        ys_ref[...] = ffn(_unpack_pairs(rows_ref[...]))

        @pl.when(e == last)
        def _():
            add_rows(last, ys_ref)
            o_ref[0] += scratch[4][...]

    @pl.when(e % 2 == 0)
    def _():
        step(scratch[0], scratch[1], scratch[2], scratch[3])

    @pl.when(e % 2 == 1)
    def _():
        step(scratch[1], scratch[0], scratch[3], scratch[2])


def _experts(h2p, slot_t, idx, aff_t, wg, wu, wd, layer, cap, group):
    b, n, half = h2p.shape
    d = 2 * half
    _, ne, _, f = wg.shape
    row_gather = idx is not None
    slot_rows = slot_t.reshape(b, ne, 1, n)
    aff_rows = aff_t.reshape(b, ne, 1, n)
    slot_tok = jnp.transpose(slot_t.reshape(b, ne, n), (0, 2, 1)).astype(F32)
    slot_tok = jnp.pad(slot_tok, ((0, 0), (0, 0), (0, LANES - ne)))
    row = pl.BlockSpec((group, 1, 1, n), lambda bb, e, idx: (bb, e, 0, 0))
    return pl.pallas_call(
        functools.partial(_experts_body, cap=cap, row_gather=row_gather),
        out_shape=jax.ShapeDtypeStruct((b, n, d), F32),
        grid_spec=pltpu.PrefetchScalarGridSpec(
            num_scalar_prefetch=1,
            grid=(b // group, ne),
            in_specs=[pl.BlockSpec((group, n, half), lambda bb, e, idx: (bb, 0, 0)), row, row,
                      pl.BlockSpec((group, n, LANES), lambda bb, e, idx: (bb, 0, 0)),
                      pl.BlockSpec((1, 1, d, f), lambda bb, e, idx: (layer, e, 0, 0)),
                      pl.BlockSpec((1, 1, d, f), lambda bb, e, idx: (layer, e, 0, 0)),
                      pl.BlockSpec((1, 1, f, d), lambda bb, e, idx: (layer, e, 0, 0))],
            out_specs=pl.BlockSpec((group, n, d), lambda bb, e, idx: (bb, 0, 0)),
            scratch_shapes=([pltpu.VMEM((cap, half), jnp.uint32)] * 2 + [pltpu.VMEM((cap, d), F32)] * 2
                            + [pltpu.VMEM((n, d), F32)] if row_gather else [])),
        compiler_params=_params("arbitrary", "arbitrary"),
        name="experts",
    )(idx.reshape(-1) if row_gather else jnp.zeros((1,), jnp.int32), h2p, slot_rows, aff_rows, slot_tok, wg, wu, wd)


def _moe(h2p, aff, wg, wu, wd, layer):
    b, n, _ = h2p.shape
    cap = CAPACITY_FACTOR * n // N_EXPERTS
    group = min(b, max(1, MXU_ROWS // cap))
    while b % group:
        group -= 1
    rows = b * N_EXPERTS
    row_gather = group == 1 and rows % LANES == 0 and cap % LANES == 0
    aff_t = jnp.transpose(aff[:, :, :N_EXPERTS], (0, 2, 1)).reshape(rows, n)
    outs = _route(aff_t, cap, row_gather)
    return _experts(h2p, outs[0], outs[1] if row_gather else None, aff_t, wg, wu, wd, layer, cap, group)


def _ln2_body(x_ref, y_ref, g2_ref, lng_ref, lnb_ref, o_ref, *, alpha):
    o_ref[0] = _layer_norm(alpha * x_ref[0] + g2_ref[0] * y_ref[0], lng_ref[...], lnb_ref[...])


def _ln2(x1, y, g2, lng, lnb, alpha, tm):
    b, n, d = x1.shape
    tm = min(tm, n)
    tok = pl.BlockSpec((1, tm, d), lambda bb, i: (bb, i, 0))
    vec = pl.BlockSpec((1, d), lambda bb, i: (0, 0))
    return pl.pallas_call(
        functools.partial(_ln2_body, alpha=alpha),
        out_shape=jax.ShapeDtypeStruct((b, n, d), F32),
        grid=(b, n // tm),
        in_specs=[tok, tok, pl.BlockSpec((1, 1, d), lambda bb, i: (bb, 0, 0)), vec, vec],
        out_specs=tok,
        compiler_params=_params("arbitrary", "arbitrary"),
        name="ln2",
    )(x1, y, g2, lng, lnb)


def _rope_tables(n, enabled):
    lane = np.arange(LANES)
    if not enabled:
        one, zero = jnp.ones((n, LANES), F32), jnp.zeros((n, LANES), F32)
        return one, zero, one, zero
    t = jnp.arange(n)
    pos = jnp.stack([t // GRID_W, t % GRID_W], axis=1).astype(F32)

    def build(jj, d, active):
        freqs = ROPE_THETA ** (-jnp.arange(d, dtype=F32) / d)
        ang = pos[:, (jj // (2 * d)) % 2] * freqs[jj % d][None, :]
        sign = np.where((jj // d) % 2 == 0, -1.0, 1.0).astype(np.float32)
        cos = jnp.where(active[None, :], jnp.cos(ang), 1.0)
        sin = jnp.where(active[None, :], jnp.sin(ang) * sign[None, :], 0.0)
        return cos, sin

    cos_a, sin_a = build(lane % HEAD_DIM, HEAD_DIM // 4, np.ones(LANES, bool))
    in_rope = (lane >= B_NOPE) & (lane < B_NOPE + B_ROPE)
    cos_b, sin_b = build(np.where(in_rope, lane - B_NOPE, 0), B_ROPE // 4, in_rope)
    return cos_a, sin_a, cos_b, sin_b


def _pair_heads(w, axis, heads, groups):
    shape = w.shape
    split = shape[:axis] + (groups, heads // groups, HEAD_DIM) + shape[axis + 1:]
    return jnp.swapaxes(w.reshape(split), axis, axis + 1).reshape(shape)


def _layer_weights(w_in, b_q_norm, b_kv_norm, b_w_uq, b_w_ukv, c_q_norm, c_k_norm):
    d = w_in.shape[0]
    sizes = (A_HEADS * HEAD_DIM, A_KV_HEADS * HEAD_DIM, A_KV_HEADS * HEAD_DIM, B_Q_LORA, B_KV_LORA, B_ROPE,
             C_HEADS * HEAD_DIM, C_KV_HEADS * HEAD_DIM, C_KV_HEADS * HEAD_DIM)
    offs = np.cumsum((0,) + sizes)
    a_q, a_k, a_v, b_cq, b_ckv, b_kr, c_q, c_k, c_v = [w_in[:, offs[i]:offs[i + 1]] for i in range(9)]
    gate = w_in[:, offs[9]:]
    zeros = lambda w: jnp.zeros((d, w), w_in.dtype)
    w1 = jnp.concatenate([_pair_heads(a_q, 1, A_HEADS, A_KV_HEADS), a_k], axis=1)
    w2 = jnp.concatenate([_pair_heads(c_q, 1, C_HEADS, C_KV_HEADS), c_k], axis=1)
    w3 = jnp.concatenate([a_v, c_v], axis=1)
    w4 = jnp.concatenate([b_cq, b_ckv, zeros(B_NOPE), b_kr, zeros(B_HEAD_PAD - B_NOPE - B_ROPE)], axis=1)
    uq = b_w_uq.reshape(B_Q_LORA, B_HEADS, B_NOPE + B_ROPE)
    uq = jnp.pad(uq, ((0, 0), (0, 0), (0, B_HEAD_PAD - B_NOPE - B_ROPE))).reshape(B_Q_LORA, B_HEADS * B_HEAD_PAD)
    ukv = b_w_ukv.reshape(B_KV_LORA, B_HEADS, B_NOPE + B_V)
    uk = jnp.pad(ukv[:, :, :B_NOPE], ((0, 0), (0, 0), (0, B_HEAD_PAD - B_NOPE))).reshape(B_KV_LORA, B_HEADS * B_HEAD_PAD)
    uv = ukv[:, :, B_NOPE:].reshape(B_KV_LORA, B_HEADS * B_V)
    wuk = jnp.concatenate([uk, uv], axis=1)
    width = w2.shape[1]
    blk = np.arange(width) // HEAD_DIM
    bd = jnp.asarray((blk[:, None] == blk[None, :]).astype(np.float32) / HEAD_DIM, BF16)
    gc = jnp.concatenate([jnp.tile(c_q_norm, C_HEADS), jnp.tile(c_k_norm, C_KV_HEADS)])[None, :]
    cast = lambda a: a.astype(BF16)
    return (cast(jnp.concatenate([w1, w2, w3, w4], axis=1)), cast(uq), cast(wuk), bd,
            b_q_norm[None, :], b_kv_norm[None, :], gc), cast(gate)


def kernel(x, c, ctx, c_ctx, w_ada, b_ada, w_in, a_sink, b_q_norm, b_kv_norm, b_w_uq, b_w_ukv, c_q_norm, c_k_norm,
           w_branch, w_out, ln1_g, ln1_b, w_router, w_gate, w_up, w_down, ln2_g, ln2_b):
    bsz, n, d = x.shape
    depth = w_ada.shape[0]
    alpha = (2 * depth) ** 0.25
    aq, bo = A_HEADS * HEAD_DIM, B_HEADS * B_V
    tm = 512

    rows = -(-(bsz + 1) // 8) * 8
    cc = jnp.concatenate([c, c_ctx[None, :], jnp.zeros((rows - bsz - 1, d), c.dtype)], axis=0)
    mod = _ada(cc, w_ada, b_ada)
    tabs_lat = _rope_tables(n, True)
    tabs_ctx = _rope_tables(ctx.shape[1], False)

    b_kw = dict(hq=B_HEADS, hkv=B_HEADS, dq=B_HEAD_PAD, dv=B_V, tq=1024)
    c_kw = dict(hq=C_HEADS, hkv=C_KV_HEADS, dq=HEAD_DIM, dv=HEAD_DIM, tq=512)
    a_kw = dict(hq=A_HEADS, hkv=A_KV_HEADS, dq=HEAD_DIM, dv=HEAD_DIM, tq=256)

    ffn_bf16 = (w_gate.astype(BF16), w_up.astype(BF16), w_down.astype(BF16))
    cx = ctx
    for l in range(depth):
        last = l == depth - 1
        lat_mod = [m[:, None, :] for m in jnp.split(mod[l, :bsz], 6, axis=-1)]
        ctx_mod = [jnp.broadcast_to(m[None, None, :], (bsz, 1, d)) for m in jnp.split(mod[l, bsz], 6, axis=-1)]
        sh1, sc1, g1, sh2, sc2, g2 = lat_mod
        sh1c, sc1c, g1c, sh2c, sc2c, g2c = ctx_mod
        wts, w_gl = _layer_weights(w_in[l], b_q_norm[l], b_kv_norm[l], b_w_uq[l], b_w_ukv[l], c_q_norm[l], c_k_norm[l])
        wbr = w_branch[l].astype(BF16)
        merge_w = (w_gl, _pair_heads(wbr[:aq], 0, A_HEADS, A_KV_HEADS), wbr[aq:aq + bo],
                   _pair_heads(wbr[aq + bo:], 0, C_HEADS, C_KV_HEADS), w_out[l].astype(BF16),
                   jnp.pad(w_router[l], ((0, 0), (0, LANES - N_EXPERTS))).astype(BF16))
        ffn_w = (*ffn_bf16, l)
        ln1 = (ln1_g[l][None, :], ln1_b[l][None, :])
        ln2 = (ln2_g[l][None, :], ln2_b[l][None, :])

        qa, ka, va, qb, kb, vb, qc, kc, vc = _inproj(x, sc1, sh1, wts, tabs_lat, 2 * tm)
        cqa, cka, cva, cqb, ckb, cvb, cqc, ckc, cvc = _inproj(cx, sc1c, sh1c, wts, tabs_ctx, tm)

        o_a = _attn_a(qa, ka, va, cka, cva, a_sink[l], min(n, 256))
        o_b = _attn(qb, [(ckb, cvb), (kb, vb)], None, name="attn_b", **b_kw)
        o_c = _attn(qc, [(ckc, cvc), (kc, vc)], None, name="attn_c", **c_kw)
        x1, h2, aff = _merge(o_a, o_b, o_c, x, sc1, sh1, g1, sc2, sh2, *ln1, *merge_w, alpha, tm)
        if not last:
            oc_a = _attn(cqa, [(cka, cva)], a_sink[l], name="attn_a_ctx", **a_kw)
            oc_b = _attn(cqb, [(ckb, cvb)], None, name="attn_b_ctx", **b_kw)
            oc_c = _attn(cqc, [(ckc, cvc)], None, name="attn_c_ctx", **c_kw)
            cx1, hc2, affc = _merge(oc_a, oc_b, oc_c, cx, sc1c, sh1c, g1c, sc2c, sh2c, *ln1, *merge_w, alpha, tm)

        x = _ln2(x1, _moe(h2, aff, *ffn_w), g2, *ln2, alpha, tm)
        if not last:
            cx = _ln2(cx1, _moe(hc2, affc, *ffn_w), g2c, *ln2, alpha, tm)
    return x
```
